```python
import math
import jax
import jax.numpy as jnp
from jax import lax
import numpy as np

D_MODEL = 2048
BATCH = 16
SEQ = 2048
DEPTH = 2
DEC_BATCH = 16
DEC_SEQ = 16
PAST_LEN = 4096

CHUNK = 64
N_META = 16
ROPE_THETA = 500000.0
QBLOCK = 128
NEG_INF = -1e30
N_EVEN = (DEPTH + 1) // 2
N_ODD = DEPTH // 2
MLA_HEADS = 8
Q_LORA = 512
KV_LORA = 512
NOPE_DIM = 128
ROPE_DIM = 64
MLA_V_DIM = 128
MLA_SCALE = (NOPE_DIM + ROPE_DIM) ** -0.5
MLA_OUT = MLA_HEADS * MLA_V_DIM
DIFF_HEADS = 4
DIFF_QK_DIM = 128
DIFF_V_DIM = 2 * DIFF_QK_DIM
DIFF_ROT = DIFF_QK_DIM // 4
DIFF_SCALE = DIFF_QK_DIM ** -0.5
DIFF_QK_COLS = DIFF_HEADS * 2 * DIFF_QK_DIM
DIFF_OUT = DIFF_HEADS * DIFF_V_DIM
ATTN_SPLITS = (Q_LORA, Q_LORA + KV_LORA, Q_LORA + KV_LORA + ROPE_DIM,
               Q_LORA + KV_LORA + ROPE_DIM + DIFF_QK_COLS, Q_LORA + KV_LORA + ROPE_DIM + 2 * DIFF_QK_COLS)
ATTN_IN_COLS = ATTN_SPLITS[-1] + DIFF_OUT
ATTN_OUT = MLA_OUT + DIFF_OUT
GLA_HEADS = 4
GLA_DK = D_MODEL // 2 // GLA_HEADS
GLA_DV = D_MODEL // GLA_HEADS
GLA_GATE_RANK = 16
GLA_TAU = 16.0
GLA_CHUNK = CHUNK
GLA_QK_COLS = GLA_HEADS * GLA_DK
GLA_OUT = GLA_HEADS * GLA_DV
GLA_SPLITS = (GLA_QK_COLS, 2 * GLA_QK_COLS, 2 * GLA_QK_COLS + GLA_OUT, 2 * GLA_QK_COLS + 2 * GLA_OUT)
GLA_IN_COLS = GLA_SPLITS[-1] + GLA_GATE_RANK
N_EXPERTS = 32
TOP_K = 4
D_FF_EXPERT = D_MODEL
SWIGLU_LIMIT = 7.0
SWIGLU_ALPHA = 1.702
MOE_BLOCK = 512
DEEPNORM_ALPHA = (2 * DEPTH) ** 0.25
DEEPNORM_BETA = (8 * DEPTH) ** -0.25

kernel_name = 'streaming_mla_diff_gla_moe_step'


def layer_norm(x, g, b, eps=1e-5):
    xf = x.astype(jnp.float32)
    mu = jnp.mean(xf, -1, keepdims=True)
    var = jnp.mean(jnp.square(xf - mu), -1, keepdims=True)
    return ((xf - mu) * lax.rsqrt(var + eps) * g + b).astype(x.dtype)


def rms_norm(x, g, eps=1e-6):
    xf = x.astype(jnp.float32)
    return (xf * lax.rsqrt(jnp.mean(xf * xf, -1, keepdims=True) + eps) * g).astype(x.dtype)


def apply_rope(x, pos):
    d = x.shape[-1]
    inv = ROPE_THETA ** (-jnp.arange(0, d, 2, dtype=jnp.float32) / d)
    ang = pos.astype(jnp.float32)[:, None] * inv[None, :]
    ang = ang.reshape(ang.shape[:1] + (1,) * (x.ndim - 3) + ang.shape[1:])
    cos, sin = jnp.cos(ang), jnp.sin(ang)
    xf = x.astype(jnp.float32)
    x1, x2 = xf[..., : d // 2], xf[..., d // 2:]
    return jnp.concatenate([x1 * cos - x2 * sin, x2 * cos + x1 * sin], -1).astype(x.dtype)


def partial_rope(x, pos):
    return jnp.concatenate([apply_rope(x[..., :DIFF_ROT], pos), x[..., DIFF_ROT:]], -1)


def masked_softmax(s, mask):
    return jax.nn.softmax(jnp.where(mask, s, NEG_INF), axis=-1)


def sweep_query_blocks(fn, q_cid, *qs):
    tq = q_cid.shape[0]
    nb = -(-tq // QBLOCK)
    pad = nb * QBLOCK - tq
    cid = jnp.concatenate([q_cid, jnp.broadcast_to(q_cid[-1], (pad,))]).reshape(nb, QBLOCK)

    def split_blocks(a):
        a = jnp.pad(a, [(0, 0), (0, pad)] + [(0, 0)] * (a.ndim - 2))
        return jnp.swapaxes(a.reshape((a.shape[0], nb, QBLOCK) + a.shape[2:]), 0, 1)

    out = lax.map(lambda args: fn(*args), (cid,) + tuple(split_blocks(a) for a in qs))
    out = jnp.swapaxes(out, 0, 1)
    return out.reshape((out.shape[0], nb * QBLOCK) + out.shape[3:])[:, :tq]


def attn_pair_mixer(x, pos, q_cid, k_cid, past, w_in, q_norm, w_uq, kv_norm, w_uk, w_uv,
                    lam_vecs, subln, w_out, lambda_init):
    B, T, _ = x.shape
    q_c, kv_c, k_pe, dq, dk, dv = jnp.split(x @ w_in, ATTN_SPLITS, axis=-1)
    q = jnp.einsum('btc,chd->bthd', rms_norm(q_c, q_norm), w_uq)
    q_lat = jnp.einsum('bthd,chd->bthc', q[..., :NOPE_DIM], w_uk)
    q_pe = apply_rope(q[..., NOPE_DIM:], pos)
    ckv = rms_norm(kv_c, kv_norm)
    kpe = apply_rope(k_pe, pos)
    dq = partial_rope(dq.reshape(B, T, DIFF_HEADS, 2, DIFF_QK_DIM), pos)
    dk = partial_rope(dk.reshape(B, T, DIFF_HEADS, 2, DIFF_QK_DIM), pos)
    dv = dv.reshape(B, T, DIFF_HEADS, DIFF_V_DIM)
    if past is None:
        ckv_all, kpe_all, dk_all, dv_all = ckv, kpe, dk, dv
    else:
        ckv_all = jnp.concatenate([past[0], ckv], 1)
        kpe_all = jnp.concatenate([past[1], kpe], 1)
        dk_all = jnp.concatenate([past[2], dk], 1)
        dv_all = jnp.concatenate([past[3], dv], 1)
    lv = lam_vecs.astype(jnp.float32)
    lam = jnp.exp(jnp.sum(lv[0] * lv[1])) - jnp.exp(jnp.sum(lv[2] * lv[3])) + lambda_init

    def block(cid, ql, qp, dqb):
        mask = k_cid[None, :] <= cid[:, None]
        s_mla = (jnp.einsum('bqhc,bkc->bhqk', ql, ckv_all)
                 + jnp.einsum('bqhr,bkr->bhqk', qp, kpe_all)).astype(jnp.float32) * MLA_SCALE
        p_mla = masked_softmax(s_mla, mask).astype(x.dtype)
        o_mla = jnp.einsum('bqhc,chd->bqhd', jnp.einsum('bhqk,bkc->bqhc', p_mla, ckv_all), w_uv)
        s_dif = jnp.einsum('bqhid,bkhid->bhiqk', dqb, dk_all).astype(jnp.float32) * DIFF_SCALE
        p_dif = masked_softmax(s_dif, mask)
        a_dif = (p_dif[:, :, 0] - lam * p_dif[:, :, 1]).astype(x.dtype)
        o_dif = jnp.einsum('bhqk,bkhv->bqhv', a_dif, dv_all)
        bq = (ql.shape[0], ql.shape[1])
        return jnp.concatenate([o_mla.reshape(bq + (MLA_OUT,)).astype(x.dtype),
                                o_dif.reshape(bq + (DIFF_OUT,)).astype(x.dtype)], -1)

    o = sweep_query_blocks(block, q_cid, q_lat, q_pe, dq)
    o_dif = rms_norm(o[..., MLA_OUT:].reshape(B, T, DIFF_HEADS, DIFF_V_DIM), subln) * (1.0 - lambda_init)
    y = jnp.concatenate([o[..., :MLA_OUT], o_dif.reshape(B, T, DIFF_OUT).astype(x.dtype)], -1) @ w_out
    return y, (ckv, kpe, dk, dv)


def gla_recurrence(q, k, v, log_a, s0):
    B, T, H, DK = q.shape
    DV = v.shape[-1]
    n = -(-T // GLA_CHUNK)
    pad = n * GLA_CHUNK - T

    def chunks(a):
        a = jnp.pad(a.astype(jnp.float32), [(0, 0), (0, pad), (0, 0), (0, 0)])
        return jnp.moveaxis(a.reshape(B, n, GLA_CHUNK, H, a.shape[-1]), 1, 0)

    causal = jnp.tril(jnp.ones((GLA_CHUNK, GLA_CHUNK), bool))

    def step(S, inp):
        qc, kc, vc, gc = inp
        b = jnp.cumsum(gc, axis=1)
        bL = b[:, -1]
        q_in = qc * jnp.exp(b)
        scores = jnp.einsum('bthd,bshd->bhts', q_in, kc * jnp.exp(-b))
        scores = jnp.where(causal, scores, 0.0)
        o = jnp.einsum('bhts,bshv->bthv', scores, vc) + jnp.einsum('bthd,bhdv->bthv', q_in, S)
        S = jnp.exp(bL)[..., None] * S + jnp.einsum('bshd,bshv->bhdv', kc * jnp.exp(bL[:, None] - b), vc)
        return S, o

    S, o = lax.scan(step, s0.astype(jnp.float32), (chunks(q), chunks(k), chunks(v), chunks(log_a)))
    o = jnp.moveaxis(o, 0, 1).reshape(B, n * GLA_CHUNK, H, DV)[:, :T]
    return o, S


def gla_mixer(x, s0, w_in, w_gate, b_gate, g_norm, w_out):
    B, T, _ = x.shape
    q, k, v, r, g_lr = jnp.split(x @ w_in, GLA_SPLITS, axis=-1)
    q = q.reshape(B, T, GLA_HEADS, GLA_DK) * GLA_DK ** -0.5
    k = k.reshape(B, T, GLA_HEADS, GLA_DK)
    v = v.reshape(B, T, GLA_HEADS, GLA_DV)
    log_a = jax.nn.log_sigmoid((g_lr @ w_gate + b_gate).astype(jnp.float32)) / GLA_TAU
    o, s = gla_recurrence(q, k, v, log_a.reshape(B, T, GLA_HEADS, GLA_DK), s0)
    o = rms_norm(o, g_norm) * jax.nn.silu(r.astype(jnp.float32)).reshape(B, T, GLA_HEADS, GLA_DV)
    return o.reshape(B, T, GLA_OUT).astype(x.dtype) @ w_out, s.astype(x.dtype)


def moe_ffn(x, w_router, b_router, w_gu, b_gu, w_down, b_down):
    B, T, D = x.shape
    n = B * T
    xt = x.reshape(n, D)
    logits = (xt @ w_router + b_router).astype(jnp.float32)
    top_val, top_idx = lax.top_k(logits, TOP_K)
    gates = jax.nn.softmax(top_val, axis=-1)
    m = n * TOP_K
    flat_e = top_idx.reshape(m)
    flat_tok = jnp.arange(m, dtype=jnp.int32) // TOP_K
    order = jnp.argsort(flat_e)
    se = flat_e[order]
    counts = jnp.bincount(flat_e, length=N_EXPERTS)
    padded = (counts + MOE_BLOCK - 1) // MOE_BLOCK * MOE_BLOCK
    pad_end = jnp.cumsum(padded)
    pad_start = pad_end - padded
    start = jnp.cumsum(counts) - counts
    dest = pad_start[se] + jnp.arange(m) - start[se]
    nb = -(-m // MOE_BLOCK) + N_EXPERTS
    rows = nb * MOE_BLOCK
    row_tok = jnp.zeros((rows,), jnp.int32).at[dest].set(flat_tok[order])
    row_w = jnp.zeros((rows,), jnp.float32).at[dest].set(gates.reshape(m)[order])
    block_e = jnp.minimum(jnp.searchsorted(pad_end, jnp.arange(nb) * MOE_BLOCK, side='right'), N_EXPERTS - 1)

    def expert_block(args):
        tok, wgt, e = args
        hh = (xt[tok] @ w_gu[e] + b_gu[e]).astype(jnp.float32)
        gate = jnp.minimum(hh[:, :D_FF_EXPERT], SWIGLU_LIMIT)
        up = jnp.clip(hh[:, D_FF_EXPERT:], -SWIGLU_LIMIT, SWIGLU_LIMIT)
        act = (gate * jax.nn.sigmoid(SWIGLU_ALPHA * gate) * (up + 1.0)).astype(x.dtype)
        return (act @ w_down[e] + b_down[e]).astype(jnp.float32) * wgt[:, None]

    out = lax.map(expert_block, (row_tok.reshape(nb, MOE_BLOCK), row_w.reshape(nb, MOE_BLOCK), block_e))
    y = jnp.zeros((n, D), jnp.float32).at[row_tok].add(out.reshape(rows, D))
    return y.astype(x.dtype).reshape(B, T, D)


def run_trunk(x, pos, q_cid, k_cid, attn_past, gla_past, prm):
    new_rows, new_states = [], []
    for l in range(DEPTH):
        i = l // 2
        if l % 2 == 0:
            past = None if attn_past is None else tuple(c[i] for c in attn_past)
            h, rows = attn_pair_mixer(x, pos, q_cid, k_cid, past, prm['w_in_attn'][i], prm['mla_q_norm'][i],
                                      prm['mla_w_uq'][i], prm['mla_kv_norm'][i], prm['mla_w_uk'][i],
                                      prm['mla_w_uv'][i], prm['diff_lambda'][i], prm['diff_subln'][i],
                                      prm['w_out_attn'][i], 0.8 - 0.6 * math.exp(-0.3 * l))
            new_rows.append(rows)
        else:
            s0 = (jnp.zeros((x.shape[0], GLA_HEADS, GLA_DK, GLA_DV), jnp.float32)
                  if gla_past is None else gla_past[i])
            h, s = gla_mixer(x, s0, prm['w_in_gla'][i], prm['gla_w_gate'][i], prm['gla_b_gate'][i],
                             prm['gla_norm'][i], prm['w_out_gla'][i])
            new_states.append(s)
        x = layer_norm(DEEPNORM_ALPHA * x + h, prm['ln1_g'][l], prm['ln1_b'][l])
        f = moe_ffn(x, prm['moe_w_router'][l], prm['moe_b_router'][l], prm['moe_w_gu'][l],
                    prm['moe_b_gu'][l], prm['moe_w_down'][l], prm['moe_b_down'][l])
        x = layer_norm(DEEPNORM_ALPHA * x + f, prm['ln2_g'][l], prm['ln2_b'][l])
    rows = tuple(jnp.stack([r[j] for r in new_rows]) for j in range(4))
    return x, rows, jnp.stack(new_states)


def setup_inputs(seed: int = 0) -> dict:
    key = jax.random.key(seed)
    ks = iter(jax.random.split(key, 64))

    def nrm(shape, scale):
        return jax.random.normal(next(ks), shape, jnp.float32) * scale

    return {
        'x_prompt': nrm((BATCH, SEQ, D_MODEL), 1.0),
        'x_sample': nrm((DEC_BATCH, DEC_SEQ, D_MODEL), 1.0),
        'cache_mla_ckv': nrm((N_EVEN, DEC_BATCH, PAST_LEN, KV_LORA), 1.0),
        'cache_mla_kpe': nrm((N_EVEN, DEC_BATCH, PAST_LEN, ROPE_DIM), 1.0),
        'cache_diff_k': nrm((N_EVEN, DEC_BATCH, PAST_LEN, DIFF_HEADS, 2, DIFF_QK_DIM), 1.0),
        'cache_diff_v': nrm((N_EVEN, DEC_BATCH, PAST_LEN, DIFF_HEADS, DIFF_V_DIM), 1.0),
        'state_gla': nrm((N_ODD, DEC_BATCH, GLA_HEADS, GLA_DK, GLA_DV), 1.0),
        'meta_tokens': nrm((N_META, D_MODEL), 1.0),
        'w_in_attn': nrm((N_EVEN, D_MODEL, ATTN_IN_COLS), D_MODEL ** -0.5),
        'mla_q_norm': 1.0 + nrm((N_EVEN, Q_LORA), 0.01),
        'mla_w_uq': nrm((N_EVEN, Q_LORA, MLA_HEADS, NOPE_DIM + ROPE_DIM), Q_LORA ** -0.5),
        'mla_kv_norm': 1.0 + nrm((N_EVEN, KV_LORA), 0.01),
        'mla_w_uk': nrm((N_EVEN, KV_LORA, MLA_HEADS, NOPE_DIM), KV_LORA ** -0.5),
        'mla_w_uv': nrm((N_EVEN, KV_LORA, MLA_HEADS, MLA_V_DIM), KV_LORA ** -0.5),
        'diff_lambda': nrm((N_EVEN, 4, DIFF_QK_DIM), 0.1),
        'diff_subln': 1.0 + nrm((N_EVEN, DIFF_V_DIM), 0.01),
        'w_out_attn': nrm((N_EVEN, ATTN_OUT, D_MODEL), ATTN_OUT ** -0.5 * DEEPNORM_BETA),
        'w_in_gla': nrm((N_ODD, D_MODEL, GLA_IN_COLS), D_MODEL ** -0.5),
        'gla_w_gate': nrm((N_ODD, GLA_GATE_RANK, GLA_QK_COLS), GLA_GATE_RANK ** -0.5),
        'gla_b_gate': nrm((N_ODD, GLA_QK_COLS), 0.1),
        'gla_norm': 1.0 + nrm((N_ODD, GLA_DV), 0.01),
        'w_out_gla': nrm((N_ODD, GLA_OUT, D_MODEL), GLA_OUT ** -0.5 * DEEPNORM_BETA),
        'ln1_g': 1.0 + nrm((DEPTH, D_MODEL), 0.01),
        'ln1_b': nrm((DEPTH, D_MODEL), 0.01),
        'ln2_g': 1.0 + nrm((DEPTH, D_MODEL), 0.01),
        'ln2_b': nrm((DEPTH, D_MODEL), 0.01),
        'moe_w_router': nrm((DEPTH, D_MODEL, N_EXPERTS), D_MODEL ** -0.5),
        'moe_b_router': nrm((DEPTH, N_EXPERTS), 0.01),
        'moe_w_gu': nrm((DEPTH, N_EXPERTS, D_MODEL, 2 * D_FF_EXPERT), D_MODEL ** -0.5),
        'moe_b_gu': nrm((DEPTH, N_EXPERTS, 2 * D_FF_EXPERT), 0.01),
        'moe_w_down': nrm((DEPTH, N_EXPERTS, D_FF_EXPERT, D_MODEL), D_FF_EXPERT ** -0.5 * DEEPNORM_BETA),
        'moe_b_down': nrm((DEPTH, N_EXPERTS, D_MODEL), 0.01),
    }


def reference(x_prompt, x_sample, cache_mla_ckv, cache_mla_kpe, cache_diff_k, cache_diff_v, state_gla,
              meta_tokens, w_in_attn, mla_q_norm, mla_w_uq, mla_kv_norm, mla_w_uk, mla_w_uv,
              diff_lambda, diff_subln, w_out_attn, w_in_gla, gla_w_gate, gla_b_gate, gla_norm, w_out_gla,
              ln1_g, ln1_b, ln2_g, ln2_b, moe_w_router, moe_b_router, moe_w_gu, moe_b_gu,
              moe_w_down, moe_b_down):
    prm = {'w_in_attn': w_in_attn, 'mla_q_norm': mla_q_norm, 'mla_w_uq': mla_w_uq,
           'mla_kv_norm': mla_kv_norm, 'mla_w_uk': mla_w_uk, 'mla_w_uv': mla_w_uv,
           'diff_lambda': diff_lambda, 'diff_subln': diff_subln, 'w_out_attn': w_out_attn,
           'w_in_gla': w_in_gla, 'gla_w_gate': gla_w_gate, 'gla_b_gate': gla_b_gate,
           'gla_norm': gla_norm, 'w_out_gla': w_out_gla, 'ln1_g': ln1_g, 'ln1_b': ln1_b,
           'ln2_g': ln2_g, 'ln2_b': ln2_b, 'moe_w_router': moe_w_router, 'moe_b_router': moe_b_router,
           'moe_w_gu': moe_w_gu, 'moe_b_gu': moe_b_gu, 'moe_w_down': moe_w_down, 'moe_b_down': moe_b_down}
    b = x_prompt.shape[0]
    t_p = N_META + x_prompt.shape[1]
    meta = jnp.broadcast_to(meta_tokens.astype(x_prompt.dtype), (b, N_META, D_MODEL))
    xp = jnp.concatenate([meta, x_prompt], 1)
    pos_p = jnp.arange(t_p, dtype=jnp.int32)
    cid_p = jnp.where(pos_p < N_META, 0, 1 + (pos_p - N_META) // CHUNK)
    yp, rows_p, gla_p = run_trunk(xp, pos_p, cid_p, cid_p, None, None, prm)
    ckv_p, kpe_p, dk_p, dv_p = rows_p
    past_len = cache_mla_ckv.shape[2]
    dec_seq = x_sample.shape[1]
    pos_s = past_len + jnp.arange(dec_seq, dtype=jnp.int32)
    q_cid_s = jnp.ones((dec_seq,), jnp.int32)
    k_cid_s = jnp.concatenate([jnp.zeros((past_len,), jnp.int32), q_cid_s])
    ys, rows_s, gla_s = run_trunk(x_sample, pos_s, q_cid_s, k_cid_s,
                                  (cache_mla_ckv, cache_mla_kpe, cache_diff_k, cache_diff_v), state_gla, prm)
    ckv_s, kpe_s, dk_s, dv_s = rows_s
    return (yp[:, N_META:], ys, ckv_p, kpe_p, dk_p, dv_p, gla_p, ckv_s, kpe_s, dk_s, dv_s, gla_s)
```

```python
import functools
import math

import jax
import jax.numpy as jnp
from jax import lax
from jax.experimental import pallas as pl
from jax.experimental.pallas import tpu as pltpu

f32 = jnp.float32
bf16 = jnp.bfloat16
i32 = jnp.int32

CHUNK = 64
ROPE_THETA = 500000.0
GLA_TAU = 16.0
GLA_CHUNK = 64
TOP_K = 4
SWIGLU_LIMIT = 7.0
SWIGLU_ALPHA = 1.702
DIFF_ROT = 32
NEG_BIG = -1e30

LANES = 128
VMEM_LIMIT_BYTES = 56 * 1024 * 1024


def _params(*sem):
    return pltpu.CompilerParams(dimension_semantics=sem, vmem_limit_bytes=VMEM_LIMIT_BYTES)


def _tile(n, prefs):
    for t in prefs:
        if n % t == 0:
            return t
    raise ValueError(f"no tile in {prefs} divides {n}")


def _dot(a, b, precision=None):
    return jnp.dot(a, b, preferred_element_type=f32, precision=precision)


def _dot_nt(a, b, precision=None):
    return lax.dot_general(a, b, (((1,), (1,)), ((), ())), preferred_element_type=f32, precision=precision)


def _rms(x, g, eps=1e-6):
    return x * lax.rsqrt(jnp.mean(x * x, -1, keepdims=True) + eps) * g


def _layer_norm(x, g, b, eps=1e-5):
    mu = jnp.mean(x, -1, keepdims=True)
    xc = x - mu
    var = jnp.mean(xc * xc, -1, keepdims=True)
    return xc * lax.rsqrt(var + eps) * g + b


def _mm_body(x_ref, w_ref, o_ref):
    o_ref[...] = _dot(x_ref[...].astype(bf16), w_ref[...].astype(bf16)).astype(o_ref.dtype)


def _matmul(x, w, out_dtype, name):
    m, k = x.shape
    n = w.shape[1]
    tm = _tile(m, (512, 256, 128, 64, 32, 16, 8))
    tn = _tile(n, (1408, 1024, 896, 768, 640, 512, 384, 256, 128))
    return pl.pallas_call(
        _mm_body,
        grid=(m // tm, n // tn),
        in_specs=[pl.BlockSpec((tm, k), lambda i, j: (i, 0)),
                  pl.BlockSpec((k, tn), lambda i, j: (0, j))],
        out_specs=pl.BlockSpec((tm, tn), lambda i, j: (i, j)),
        out_shape=jax.ShapeDtypeStruct((m, n), out_dtype),
        compiler_params=_params("parallel", "parallel"),
        name=name,
    )(x, w)


def _attn_prep_body(y_ref, mc_ref, ms_ref, dc_ref, ds_ref, qn_ref, kvn_ref, wuq_ref, wuk_ref,
                    qm_ref, ckv_ref, kpe_ref, km_ref, dqb_ref, dk_ref, dkb_ref, dv_ref, dvb_ref,
                    *, n_heads, q_lora, kv_lora, nope, rope, diff_cols):
    tm = y_ref.shape[0]
    lane = lax.broadcasted_iota(i32, (tm, LANES), 1)
    mc, ms, dc, ds = mc_ref[...], ms_ref[...], dc_ref[...], ds_ref[...]
    half = rope // 2

    def rope_pair(v):
        rolled = jnp.where((lane % rope) < half, pltpu.roll(v, LANES - half, 1), pltpu.roll(v, half, 1))
        return v * mc + rolled * ms

    dhalf = DIFF_ROT // 2

    def rope_diff(v):
        rolled = jnp.where(lane < dhalf, pltpu.roll(v, LANES - dhalf, 1), pltpu.roll(v, dhalf, 1))
        return v * dc + rolled * ds

    qn = _rms(y_ref[:, 0:q_lora], qn_ref[...]).astype(bf16)
    q = _dot(qn, wuq_ref[...])
    for h in range(n_heads):
        lat = _dot(q[:, h * nope:(h + 1) * nope].astype(bf16), wuk_ref[h])
        qm_ref[h, :, 0:kv_lora] = lat.astype(bf16)
    base = n_heads * nope
    for g in range(n_heads // 2):
        pe = rope_pair(q[:, base + g * LANES: base + (g + 1) * LANES]).astype(bf16)
        qm_ref[2 * g, :, kv_lora:kv_lora + rope] = pe[:, 0:rope]
        qm_ref[2 * g + 1, :, kv_lora:kv_lora + rope] = pe[:, rope:2 * rope]
    ckv = _rms(y_ref[:, q_lora:q_lora + kv_lora], kvn_ref[...])
    ckv_ref[...] = ckv
    km_ref[:, 0:kv_lora] = ckv.astype(bf16)
    o_dq = q_lora + kv_lora
    o_dk = o_dq + diff_cols
    o_dv = o_dk + diff_cols
    o_pe = o_dv + diff_cols
    kpe = rope_pair(y_ref[:, o_pe:o_pe + LANES])
    kpe_ref[...] = kpe
    km_ref[:, kv_lora:kv_lora + rope] = kpe[:, 0:rope].astype(bf16)
    for g in range(diff_cols // LANES):
        dq = rope_diff(y_ref[:, o_dq + g * LANES:o_dq + (g + 1) * LANES])
        dqb_ref[:, g * LANES:(g + 1) * LANES] = dq.astype(bf16)
        dk = rope_diff(y_ref[:, o_dk + g * LANES:o_dk + (g + 1) * LANES])
        dk_ref[:, g * LANES:(g + 1) * LANES] = dk
        dkb_ref[:, g * LANES:(g + 1) * LANES] = dk.astype(bf16)
    dv = y_ref[:, o_dv:o_dv + diff_cols]
    dv_ref[...] = dv
    dvb_ref[...] = dv.astype(bf16)


def _attn_prep(y, tabs, q_norm, kv_norm, wuq, wukT, *, n_heads, q_lora, kv_lora, nope, rope, diff_cols):
    r, ycols = y.shape
    tm = _tile(r, (256, 128, 64, 32, 16))
    dm = kv_lora + rope
    row = lambda i: (i, 0)
    const2 = lambda i: (0, 0)
    body = functools.partial(_attn_prep_body, n_heads=n_heads, q_lora=q_lora, kv_lora=kv_lora,
                             nope=nope, rope=rope, diff_cols=diff_cols)
    return pl.pallas_call(
        body,
        grid=(r // tm,),
        in_specs=[pl.BlockSpec((tm, ycols), row)]
                 + [pl.BlockSpec((tm, LANES), row)] * 4
                 + [pl.BlockSpec((1, q_lora), const2), pl.BlockSpec((1, kv_lora), const2),
                    pl.BlockSpec(wuq.shape, const2), pl.BlockSpec(wukT.shape, lambda i: (0, 0, 0))],
        out_specs=[pl.BlockSpec((n_heads, tm, dm), lambda i: (0, i, 0)),
                   pl.BlockSpec((tm, kv_lora), row), pl.BlockSpec((tm, LANES), row),
                   pl.BlockSpec((tm, dm), row), pl.BlockSpec((tm, diff_cols), row),
                   pl.BlockSpec((tm, diff_cols), row), pl.BlockSpec((tm, diff_cols), row),
                   pl.BlockSpec((tm, diff_cols), row), pl.BlockSpec((tm, diff_cols), row)],
        out_shape=[jax.ShapeDtypeStruct((n_heads, r, dm), bf16),
                   jax.ShapeDtypeStruct((r, kv_lora), f32), jax.ShapeDtypeStruct((r, LANES), f32),
                   jax.ShapeDtypeStruct((r, dm), bf16), jax.ShapeDtypeStruct((r, diff_cols), bf16),
                   jax.ShapeDtypeStruct((r, diff_cols), f32), jax.ShapeDtypeStruct((r, diff_cols), bf16),
                   jax.ShapeDtypeStruct((r, diff_cols), f32), jax.ShapeDtypeStruct((r, diff_cols), bf16)],
        compiler_params=_params("parallel"),
        name="attn_prep",
    )(y, *tabs, q_norm, kv_norm, wuq, wukT)


def _chunk_mask(n_rows, tq, tk):
    rows = lax.broadcasted_iota(i32, (n_rows, tk), 0)
    cols = lax.broadcasted_iota(i32, (n_rows, tk), 1)
    return (cols // CHUNK) <= (lax.rem(rows, tq) // CHUNK)


def _mla_flash_body(q_ref, kf_ref, kmeta_ref, wuv_ref, o_ref, m_ref, l_ref, acc_ref, *, scale, kv_lora):
    n_heads, tq, dm = q_ref.shape
    tk = kf_ref.shape[0]
    vdim = wuv_ref.shape[2]
    qi, kj = pl.program_id(1), pl.program_id(2)
    q = q_ref[...].reshape(n_heads * tq, dm)

    def update(s, v):
        m_prev = m_ref[...]
        m_new = jnp.maximum(m_prev, jnp.max(s, -1, keepdims=True))
        a = jnp.exp(m_prev - m_new)
        p = jnp.exp(s - m_new)
        l_ref[...] = a * l_ref[...] + jnp.sum(p, -1, keepdims=True)
        acc_ref[...] = a * acc_ref[...] + _dot(p.astype(bf16), v)
        m_ref[...] = m_new

    @pl.when(kj == 0)
    def _():
        m_ref[...] = jnp.full(m_ref.shape, NEG_BIG, f32)
        l_ref[...] = jnp.zeros(l_ref.shape, f32)
        acc_ref[...] = jnp.zeros(acc_ref.shape, f32)
        km = kmeta_ref[...]
        update(_dot_nt(q, km) * scale, km[:, 0:kv_lora])

    @pl.when(kj < qi)
    def _():
        kf = kf_ref[...]
        update(_dot_nt(q, kf) * scale, kf[:, 0:kv_lora])

    @pl.when(kj == qi)
    def _():
        kf = kf_ref[...]
        s = jnp.where(_chunk_mask(n_heads * tq, tq, tk), _dot_nt(q, kf) * scale, NEG_BIG)
        update(s, kf[:, 0:kv_lora])

    @pl.when(kj == pl.num_programs(2) - 1)
    def _():
        o = (acc_ref[...] / l_ref[...]).astype(bf16)
        for h in range(n_heads):
            o_ref[:, h * vdim:(h + 1) * vdim] = _dot(o[h * tq:(h + 1) * tq], wuv_ref[h]).astype(o_ref.dtype)


def _mla_flash(qm, km, wuv, *, n_batch, seq, n_meta, scale, kv_lora, tq):
    n_heads, r, dm = qm.shape
    vdim = wuv.shape[2]
    nq = seq // tq
    meta_blk0 = n_batch * seq // n_meta
    body = functools.partial(_mla_flash_body, scale=scale, kv_lora=kv_lora)
    return pl.pallas_call(
        body,
        grid=(n_batch, nq, nq),
        in_specs=[pl.BlockSpec((n_heads, tq, dm), lambda b, i, j: (0, b * nq + i, 0)),
                  pl.BlockSpec((tq, dm), lambda b, i, j: (b * nq + jnp.minimum(i, j), 0)),
                  pl.BlockSpec((n_meta, dm), lambda b, i, j: (meta_blk0 + b, 0)),
                  pl.BlockSpec(wuv.shape, lambda b, i, j: (0, 0, 0))],
        out_specs=pl.BlockSpec((tq, n_heads * vdim), lambda b, i, j: (b * nq + i, 0)),
        out_shape=jax.ShapeDtypeStruct((n_batch * seq, n_heads * vdim), bf16),
        scratch_shapes=[pltpu.VMEM((n_heads * tq, 1), f32), pltpu.VMEM((n_heads * tq, 1), f32),
                        pltpu.VMEM((n_heads * tq, kv_lora), f32)],
        compiler_params=_params("parallel", "parallel", "arbitrary"),
        name="mla_flash",
    )(qm, km, km, wuv)


def _diff_lambda(lam_ref, lambda_init):
    lv = lam_ref[...]
    return (jnp.exp(jnp.sum(lv[0:1] * lv[1:2], keepdims=True))
            - jnp.exp(jnp.sum(lv[2:3] * lv[3:4], keepdims=True)) + lambda_init)


def _diff_flash_body(q_ref, kf_ref, vf_ref, kmeta_ref, vmeta_ref, lam_ref, sub_ref, o_ref,
                     m_ref, l_ref, acc_ref, *, scale, lambda_init):
    tq, two_d = q_ref.shape
    d = two_d // 2
    tk = kf_ref.shape[0]
    qi, kj = pl.program_id(2), pl.program_id(3)

    def update(k, v, masked):
        for i in range(2):
            s = _dot_nt(q_ref[:, i * d:(i + 1) * d], k[:, i * d:(i + 1) * d]) * scale
            if masked:
                s = jnp.where(_chunk_mask(tq, tq, k.shape[0]), s, NEG_BIG)
            m_prev = m_ref[i]
            m_new = jnp.maximum(m_prev, jnp.max(s, -1, keepdims=True))
            a = jnp.exp(m_prev - m_new)
            p = jnp.exp(s - m_new)
            l_ref[i] = a * l_ref[i] + jnp.sum(p, -1, keepdims=True)
            acc_ref[i] = a * acc_ref[i] + _dot(p.astype(bf16), v)
            m_ref[i] = m_new

    @pl.when(kj == 0)
    def _():
        m_ref[...] = jnp.full(m_ref.shape, NEG_BIG, f32)
        l_ref[...] = jnp.zeros(l_ref.shape, f32)
        acc_ref[...] = jnp.zeros(acc_ref.shape, f32)
        update(kmeta_ref[...], vmeta_ref[...], False)

    @pl.when(kj < qi)
    def _():
        update(kf_ref[...], vf_ref[...], False)

    @pl.when(kj == qi)
    def _():
        update(kf_ref[...], vf_ref[...], True)

    @pl.when(kj == pl.num_programs(3) - 1)
    def _():
        lam = _diff_lambda(lam_ref, lambda_init)
        o = acc_ref[0] / l_ref[0] - lam * (acc_ref[1] / l_ref[1])
        o_ref[...] = (_rms(o, sub_ref[...]) * (1.0 - lambda_init)).astype(o_ref.dtype)


def _diff_flash(dqb, dkb, dvb, lam_vecs, subln, *, n_batch, seq, n_meta, n_heads, scale, lambda_init, tq):
    r, cols = dqb.shape
    hw = cols // n_heads
    nq = seq // tq
    meta_blk0 = n_batch * seq // n_meta
    body = functools.partial(_diff_flash_body, scale=scale, lambda_init=lambda_init)
    qmap = lambda b, h, i, j: (b * nq + i, h)
    kmap = lambda b, h, i, j: (b * nq + jnp.minimum(i, j), h)
    mmap = lambda b, h, i, j: (meta_blk0 + b, h)
    return pl.pallas_call(
        body,
        grid=(n_batch, n_heads, nq, nq),
        in_specs=[pl.BlockSpec((tq, hw), qmap), pl.BlockSpec((tq, hw), kmap), pl.BlockSpec((tq, hw), kmap),
                  pl.BlockSpec((n_meta, hw), mmap), pl.BlockSpec((n_meta, hw), mmap),
                  pl.BlockSpec(lam_vecs.shape, lambda b, h, i, j: (0, 0)),
                  pl.BlockSpec(subln.shape, lambda b, h, i, j: (0, 0))],
        out_specs=pl.BlockSpec((tq, hw), qmap),
        out_shape=jax.ShapeDtypeStruct((n_batch * seq, cols), bf16),
        scratch_shapes=[pltpu.VMEM((2, tq, 1), f32), pltpu.VMEM((2, tq, 1), f32), pltpu.VMEM((2, tq, hw), f32)],
        compiler_params=_params("parallel", "parallel", "parallel", "arbitrary"),
        name="diff_flash",
    )(dqb, dkb, dvb, dkb, dvb, lam_vecs, subln)


def _mla_small_body(*refs, scale, kv_lora, has_past):
    if has_past:
        q_ref, kown_ref, ckvp_ref, kpep_ref, wuv_ref, o_ref = refs
    else:
        q_ref, kown_ref, wuv_ref, o_ref = refs
    n_heads, tq, dm = q_ref.shape
    vdim = wuv_ref.shape[2]
    q = q_ref[...].reshape(n_heads * tq, dm)
    kown = kown_ref[...]
    s_own = _dot_nt(q, kown) * scale
    m = jnp.max(s_own, -1, keepdims=True)
    if has_past:
        kp = ckvp_ref[0].astype(bf16)
        s_past = (_dot_nt(q[:, 0:kv_lora], kp) + _dot_nt(q[:, kv_lora:dm], kpep_ref[0].astype(bf16))) * scale
        m = jnp.maximum(m, jnp.max(s_past, -1, keepdims=True))
    p_own = jnp.exp(s_own - m)
    l = jnp.sum(p_own, -1, keepdims=True)
    acc = _dot(p_own.astype(bf16), kown[:, 0:kv_lora])
    if has_past:
        p_past = jnp.exp(s_past - m)
        l = l + jnp.sum(p_past, -1, keepdims=True)
        acc = acc + _dot(p_past.astype(bf16), kp)
    o = (acc / l).astype(bf16)
    for h in range(n_heads):
        o_ref[:, h * vdim:(h + 1) * vdim] = _dot(o[h * tq:(h + 1) * tq], wuv_ref[h]).astype(o_ref.dtype)


def _mla_small(qm, km, wuv, past, *, n_seq, tq, row0, scale, kv_lora):
    n_heads, r, dm = qm.shape
    vdim = wuv.shape[2]
    blk0 = row0 // tq
    has_past = past is not None
    in_specs = [pl.BlockSpec((n_heads, tq, dm), lambda b: (0, blk0 + b, 0)),
                pl.BlockSpec((tq, dm), lambda b: (blk0 + b, 0))]
    args = [qm, km]
    if has_past:
        ckvp, kpep = past
        in_specs += [pl.BlockSpec((1,) + ckvp.shape[1:], lambda b: (b, 0, 0)),
                     pl.BlockSpec((1,) + kpep.shape[1:], lambda b: (b, 0, 0))]
        args += [ckvp, kpep]
    in_specs.append(pl.BlockSpec(wuv.shape, lambda b: (0, 0, 0)))
    args.append(wuv)
    body = functools.partial(_mla_small_body, scale=scale, kv_lora=kv_lora, has_past=has_past)
    return pl.pallas_call(
        body,
        grid=(n_seq,),
        in_specs=in_specs,
        out_specs=pl.BlockSpec((tq, n_heads * vdim), lambda b: (b, 0)),
        out_shape=jax.ShapeDtypeStruct((n_seq * tq, n_heads * vdim), bf16),
        compiler_params=_params("parallel"),
        name="mla_small_past" if has_past else "mla_small",
    )(*args)


def _diff_small_body(*refs, scale, lambda_init, has_past):
    if has_past:
        q_ref, kown_ref, vown_ref, kp_ref, vp_ref, lam_ref, sub_ref, o_ref = refs
    else:
        q_ref, kown_ref, vown_ref, lam_ref, sub_ref, o_ref = refs
    d = q_ref.shape[1] // 2
    outs = []
    for i in range(2):
        q = q_ref[:, i * d:(i + 1) * d]
        s_own = _dot_nt(q, kown_ref[:, i * d:(i + 1) * d]) * scale
        m = jnp.max(s_own, -1, keepdims=True)
        if has_past:
            s_past = _dot_nt(q, kp_ref[0, :, i * d:(i + 1) * d].astype(bf16)) * scale
            m = jnp.maximum(m, jnp.max(s_past, -1, keepdims=True))
        p_own = jnp.exp(s_own - m)
        l = jnp.sum(p_own, -1, keepdims=True)
        acc = _dot(p_own.astype(bf16), vown_ref[...])
        if has_past:
            p_past = jnp.exp(s_past - m)
            l = l + jnp.sum(p_past, -1, keepdims=True)
            acc = acc + _dot(p_past.astype(bf16), vp_ref[0].astype(bf16))
        outs.append(acc / l)
    o = outs[0] - _diff_lambda(lam_ref, lambda_init) * outs[1]
    o_ref[...] = (_rms(o, sub_ref[...]) * (1.0 - lambda_init)).astype(o_ref.dtype)


def _diff_small(dqb, dkb, dvb, lam_vecs, subln, past, *, n_seq, tq, row0, n_heads, scale, lambda_init):
    r, cols = dqb.shape
    hw = cols // n_heads
    blk0 = row0 // tq
    has_past = past is not None
    own = lambda b, h: (blk0 + b, h)
    in_specs = [pl.BlockSpec((tq, hw), own)] * 3
    args = [dqb, dkb, dvb]
    if has_past:
        kp, vp = past
        in_specs += [pl.BlockSpec((1, kp.shape[1], hw), lambda b, h: (b, 0, h)),
                     pl.BlockSpec((1, vp.shape[1], hw), lambda b, h: (b, 0, h))]
        args += [kp, vp]
    in_specs += [pl.BlockSpec(lam_vecs.shape, lambda b, h: (0, 0)), pl.BlockSpec(subln.shape, lambda b, h: (0, 0))]
    args += [lam_vecs, subln]
    body = functools.partial(_diff_small_body, scale=scale, lambda_init=lambda_init, has_past=has_past)
    return pl.pallas_call(
        body,
        grid=(n_seq, n_heads),
        in_specs=in_specs,
        out_specs=pl.BlockSpec((tq, hw), lambda b, h: (b, h)),
        out_shape=jax.ShapeDtypeStruct((n_seq * tq, cols), bf16),
        compiler_params=_params("parallel", "parallel"),
        name="diff_small_past" if has_past else "diff_small",
    )(*args)


def _gla_body(q_ref, k_ref, v_ref, r_ref, g_ref, wg_ref, bg_ref, gn_ref, s0_ref, o_ref, sout_ref, s_ref,
              *, n_heads):
    c = q_ref.shape[0]
    dk = q_ref.shape[1] // n_heads
    dv = v_ref.shape[1] // n_heads
    step = pl.program_id(1)

    @pl.when(step == 0)
    def _():
        s_ref[...] = s0_ref[0]

    pre = _dot(g_ref[...], wg_ref[...], precision=lax.Precision.HIGHEST) + bg_ref[...]
    log_a = jax.nn.log_sigmoid(pre) / GLA_TAU
    rows = lax.broadcasted_iota(i32, (c, c), 0)
    cols = lax.broadcasted_iota(i32, (c, c), 1)
    causal = rows >= cols
    tril = causal.astype(f32)
    for h in range(n_heads):
        la = log_a[:, h * dk:(h + 1) * dk]
        b = _dot(tril, la, precision=lax.Precision.HIGHEST)
        b_t = b.T
        b_last = b_t[:, c - 1:c]
        qh = q_ref[:, h * dk:(h + 1) * dk] * (dk ** -0.5)
        kh = k_ref[:, h * dk:(h + 1) * dk]
        vh = v_ref[:, h * dv:(h + 1) * dv].astype(bf16)
        q_in = (qh * jnp.exp(b)).astype(bf16)
        k_out = (kh * jnp.exp(-b)).astype(bf16)
        scores = jnp.where(causal, _dot_nt(q_in, k_out), 0.0)
        s_prev = s_ref[h]
        o = _dot(scores.astype(bf16), vh) + _dot(q_in, s_prev.astype(bf16))
        k_dec_t = (kh.T * jnp.exp(b_last - b_t)).astype(bf16)
        s_ref[h] = jnp.exp(b_last) * s_prev + _dot(k_dec_t, vh)
        gate = jax.nn.silu(r_ref[:, h * dv:(h + 1) * dv])
        o_ref[:, h * dv:(h + 1) * dv] = (_rms(o, gn_ref[...]) * gate).astype(o_ref.dtype)

    @pl.when(step == pl.num_programs(1) - 1)
    def _():
        sout_ref[0] = s_ref[...]


def _gla(y, w_gate, b_gate, g_norm, s0, *, n_seq, n_chunks, chunk, row0, n_heads, qk_cols, v_cols):
    blk0 = row0 // chunk
    rmap = lambda col: (lambda b, s: (blk0 + b * n_chunks + s, col))
    g_col = (2 * qk_cols + 2 * v_cols) // LANES
    const2 = lambda b, s: (0, 0)
    smap = lambda b, s: (b, 0, 0, 0)
    body = functools.partial(_gla_body, n_heads=n_heads)
    return pl.pallas_call(
        body,
        grid=(n_seq, n_chunks),
        in_specs=[pl.BlockSpec((chunk, qk_cols), rmap(0)), pl.BlockSpec((chunk, qk_cols), rmap(1)),
                  pl.BlockSpec((chunk, v_cols), rmap(2 * qk_cols // v_cols)),
                  pl.BlockSpec((chunk, v_cols), rmap(2 * qk_cols // v_cols + 1)),
                  pl.BlockSpec((chunk, LANES), rmap(g_col)),
                  pl.BlockSpec(w_gate.shape, const2), pl.BlockSpec(b_gate.shape, const2),
                  pl.BlockSpec(g_norm.shape, const2),
                  pl.BlockSpec((1,) + s0.shape[1:], smap)],
        out_specs=[pl.BlockSpec((chunk, v_cols), lambda b, s: (b * n_chunks + s, 0)),
                   pl.BlockSpec((1,) + s0.shape[1:], smap)],
        out_shape=[jax.ShapeDtypeStruct((n_seq * n_chunks * chunk, v_cols), bf16),
                   jax.ShapeDtypeStruct(s0.shape, f32)],
        scratch_shapes=[pltpu.VMEM(s0.shape[1:], f32)],
        compiler_params=_params("parallel", "arbitrary"),
        name=f"gla_chunk{chunk}",
    )(y, y, y, y, y, w_gate, b_gate, g_norm, s0)


def _proj_ln_body(*refs, n_in, alpha):
    a_refs = refs[:n_in]
    w_refs = refs[n_in:2 * n_in]
    x_ref, g_ref, b_ref, o_ref = refs[2 * n_in:]
    h = _dot(a_refs[0][...], w_refs[0][...])
    for a_ref, w_ref in zip(a_refs[1:], w_refs[1:]):
        h = h + _dot(a_ref[...], w_ref[...])
    o_ref[...] = _layer_norm(alpha * x_ref[...] + h, g_ref[...], b_ref[...])


def _proj_ln(acts, weights, x, g, b, *, alpha):
    r, d = x.shape
    tm = _tile(r, (256, 128, 64, 32, 16, 8))
    row = lambda i: (i, 0)
    const2 = lambda i: (0, 0)
    body = functools.partial(_proj_ln_body, n_in=len(acts), alpha=alpha)
    return pl.pallas_call(
        body,
        grid=(r // tm,),
        in_specs=[pl.BlockSpec((tm, a.shape[1]), row) for a in acts]
                 + [pl.BlockSpec(w.shape, const2) for w in weights]
                 + [pl.BlockSpec((tm, d), row), pl.BlockSpec((1, d), const2), pl.BlockSpec((1, d), const2)],
        out_specs=pl.BlockSpec((tm, d), row),
        out_shape=jax.ShapeDtypeStruct((r, d), f32),
        compiler_params=_params("parallel"),
        name="proj_ln",
    )(*acts, *weights, x, g, b)


def _router_body(x_ref, w_ref, b_ref, idx_ref, gate_ref, rank_ref, cnt_ref, base_ref):
    n_exp = w_ref.shape[0]
    tm = x_ref.shape[0]

    @pl.when(pl.program_id(0) == 0)
    def _():
        base_ref[...] = jnp.zeros(base_ref.shape, f32)

    logits = _dot_nt(w_ref[...], x_ref[...], precision=lax.Precision.HIGHEST) + b_ref[...]
    eid = lax.broadcasted_iota(i32, (n_exp, tm), 0).astype(f32)
    work = logits
    vals, hots = [], []
    for k in range(TOP_K):
        m = jnp.max(work, 0, keepdims=True)
        sel = jnp.min(jnp.where(work == m, eid, float(n_exp)), 0, keepdims=True)
        hot = eid == sel
        idx_ref[k:k + 1, :] = sel.astype(i32)
        vals.append(m)
        hots.append(hot)
        work = jnp.where(hot, -jnp.inf, work)
    exps = [jnp.exp(v - vals[0]) for v in vals]
    denom = functools.reduce(jnp.add, exps)
    for k in range(TOP_K):
        gate_ref[k:k + 1, :] = exps[k] / denom
    hot_all = functools.reduce(jnp.logical_or, hots).astype(bf16)
    t_row = lax.broadcasted_iota(i32, (tm, tm), 0)
    t_col = lax.broadcasted_iota(i32, (tm, tm), 1)
    before = (t_row < t_col).astype(bf16)
    prefix = _dot(hot_all, before) + base_ref[...]
    for k in range(TOP_K):
        rank_ref[k:k + 1, :] = jnp.sum(jnp.where(hots[k], prefix, 0.0), 0, keepdims=True).astype(i32)
    base_ref[...] = base_ref[...] + jnp.sum(hot_all.astype(f32), 1, keepdims=True)
    cnt_ref[...] = jnp.broadcast_to(base_ref[...], cnt_ref.shape)


def _router(x, w_t, b_col):
    r, d = x.shape
    n_exp = w_t.shape[0]
    tm = _tile(r, (512, 256, 128))
    col = lambda i: (0, i)
    const2 = lambda i: (0, 0)
    return pl.pallas_call(
        _router_body,
        grid=(r // tm,),
        in_specs=[pl.BlockSpec((tm, d), lambda i: (i, 0)), pl.BlockSpec((n_exp, d), const2),
                  pl.BlockSpec((n_exp, 1), const2)],
        out_specs=[pl.BlockSpec((TOP_K, tm), col), pl.BlockSpec((TOP_K, tm), col), pl.BlockSpec((TOP_K, tm), col),
                   pl.BlockSpec((n_exp, LANES), const2)],
        out_shape=[jax.ShapeDtypeStruct((TOP_K, r), i32), jax.ShapeDtypeStruct((TOP_K, r), f32),
                   jax.ShapeDtypeStruct((TOP_K, r), i32), jax.ShapeDtypeStruct((n_exp, LANES), f32)],
        scratch_shapes=[pltpu.VMEM((n_exp, 1), f32)],
        compiler_params=_params("arbitrary"),
        name="router",
    )(x, w_t, b_col)


def _row_copy(src_hbm, dst_vmem, sem, src_row, dst_row):
    return pltpu.make_async_copy(src_hbm.at[pl.ds(src_row, 1), :], dst_vmem.at[pl.ds(dst_row, 1), :], sem)


def _dispatch_body(tok_ref, x_hbm, o_ref, buf_ref, sem):
    tb = o_ref.shape[0]

    def start(i, carry):
        _row_copy(x_hbm, buf_ref, sem, tok_ref[0, 0, i], i).start()
        return carry

    lax.fori_loop(0, tb, start, 0)

    def wait(i, carry):
        _row_copy(x_hbm, buf_ref, sem, 0, i).wait()
        return carry

    lax.fori_loop(0, tb, wait, 0)
    o_ref[...] = buf_ref[...].astype(o_ref.dtype)


def _dispatch(x, row_tok, tb):
    d = x.shape[1]
    n_blk = row_tok.shape[0] // tb
    return pl.pallas_call(
        _dispatch_body,
        grid=(n_blk,),
        in_specs=[pl.BlockSpec((1, 1, tb), lambda i: (i, 0, 0), memory_space=pltpu.SMEM),
                  pl.BlockSpec(memory_space=pl.ANY)],
        out_specs=pl.BlockSpec((tb, d), lambda i: (i, 0)),
        out_shape=jax.ShapeDtypeStruct((n_blk * tb, d), bf16),
        scratch_shapes=[pltpu.VMEM((tb, d), f32), pltpu.SemaphoreType.DMA(())],
        compiler_params=_params("arbitrary"),
        name="moe_dispatch",
    )(row_tok.reshape(n_blk, 1, tb), x)


def _expert_body(be_ref, nused_ref, x_ref, wg_ref, wu_ref, bg_ref, bu_ref, wd_ref, bd_ref, o_ref, acc_ref):
    i, f = pl.program_id(0), pl.program_id(1)
    used = i < nused_ref[0]

    @pl.when(jnp.logical_and(used, f == 0))
    def _():
        acc_ref[...] = jnp.zeros(acc_ref.shape, f32)

    @pl.when(used)
    def _():
        x = x_ref[...]
        gate = jnp.minimum(_dot(x, wg_ref[...].astype(bf16)) + bg_ref[...], SWIGLU_LIMIT)
        up = jnp.clip(_dot(x, wu_ref[...].astype(bf16)) + bu_ref[...], -SWIGLU_LIMIT, SWIGLU_LIMIT)
        act = (gate * jax.nn.sigmoid(SWIGLU_ALPHA * gate) * (up + 1.0)).astype(bf16)
        acc_ref[...] += _dot(act, wd_ref[...].astype(bf16))

    @pl.when(f == pl.num_programs(1) - 1)
    def _():
        o_ref[...] = jnp.where(used, acc_ref[...] + bd_ref[...], 0.0)


def _experts(xs, block_e, n_used, w_gu, b_gu, w_down, b_down, tb):
    rows, d = xs.shape
    n_exp, _, two_f = w_gu.shape
    dff = two_f // 2
    tf = _tile(dff, (512, 256, 128))
    n_f = dff // tf
    n_blk = rows // tb

    def f_eff(i, f, nused):
        return jnp.where(i < nused[0], f, n_f - 1)

    grid_spec = pltpu.PrefetchScalarGridSpec(
        num_scalar_prefetch=2,
        grid=(n_blk, n_f),
        in_specs=[pl.BlockSpec((tb, d), lambda i, f, be, nu: (i, 0)),
                  pl.BlockSpec((None, d, tf), lambda i, f, be, nu: (be[i], 0, f_eff(i, f, nu))),
                  pl.BlockSpec((None, d, tf), lambda i, f, be, nu: (be[i], 0, n_f + f_eff(i, f, nu))),
                  pl.BlockSpec((None, 1, tf), lambda i, f, be, nu: (be[i], 0, f_eff(i, f, nu))),
                  pl.BlockSpec((None, 1, tf), lambda i, f, be, nu: (be[i], 0, n_f + f_eff(i, f, nu))),
                  pl.BlockSpec((None, tf, d), lambda i, f, be, nu: (be[i], f_eff(i, f, nu), 0)),
                  pl.BlockSpec((None, 1, d), lambda i, f, be, nu: (be[i], 0, 0))],
        out_specs=pl.BlockSpec((tb, d), lambda i, f, be, nu: (i, 0)),
        scratch_shapes=[pltpu.VMEM((tb, d), f32)],
    )
    return pl.pallas_call(
        _expert_body,
        grid_spec=grid_spec,
        out_shape=jax.ShapeDtypeStruct((rows, d), f32),
        compiler_params=_params("arbitrary", "arbitrary"),
        name="moe_experts",
    )(block_e, n_used, xs, w_gu, w_gu, b_gu.reshape(n_exp, 1, two_f), b_gu.reshape(n_exp, 1, two_f),
      w_down, b_down.reshape(n_exp, 1, d))


def _combine_body(pos_ref, ys_hbm, gate_ref, x_ref, g_ref, b_ref, o_ref, buf_ref, sem, *, alpha):
    tm = x_ref.shape[0]
    for k in range(TOP_K):
        def start(t, carry, k=k):
            _row_copy(ys_hbm, buf_ref.at[k], sem, pos_ref[0, 0, k * tm + t], t).start()
            return carry
        lax.fori_loop(0, tm, start, 0)
    for k in range(TOP_K):
        def wait(t, carry, k=k):
            _row_copy(ys_hbm, buf_ref.at[k], sem, 0, t).wait()
            return carry
        lax.fori_loop(0, tm, wait, 0)
    y = buf_ref[0] * gate_ref[:, 0:1]
    for k in range(1, TOP_K):
        y = y + buf_ref[k] * gate_ref[:, k:k + 1]
    o_ref[...] = _layer_norm(alpha * x_ref[...] + y, g_ref[...], b_ref[...])


def _combine_ln(ys, pos, gates, x, g, b, *, alpha):
    r, d = x.shape
    tm = _tile(r, (128, 64, 32, 16, 8))
    n_t = r // tm
    pos_blocks = pos.reshape(TOP_K, n_t, tm).transpose(1, 0, 2).reshape(n_t, 1, TOP_K * tm)
    row = lambda i: (i, 0)
    const2 = lambda i: (0, 0)
    body = functools.partial(_combine_body, alpha=alpha)
    return pl.pallas_call(
        body,
        grid=(n_t,),
        in_specs=[pl.BlockSpec((1, 1, TOP_K * tm), lambda i: (i, 0, 0), memory_space=pltpu.SMEM),
                  pl.BlockSpec(memory_space=pl.ANY),
                  pl.BlockSpec((tm, TOP_K), row), pl.BlockSpec((tm, d), row),
                  pl.BlockSpec((1, d), const2), pl.BlockSpec((1, d), const2)],
        out_specs=pl.BlockSpec((tm, d), row),
        out_shape=jax.ShapeDtypeStruct((r, d), f32),
        scratch_shapes=[pltpu.VMEM((TOP_K, tm, d), f32), pltpu.SemaphoreType.DMA(())],
        compiler_params=_params("arbitrary"),
        name="moe_combine_ln",
    )(pos_blocks, ys, gates, x, g, b)


def _moe_ln(x, w_router, b_router, w_gu, b_gu, w_down, b_down, g, b, *, alpha):
    r, d = x.shape
    n_exp = w_router.shape[1]
    tb = 512 if r * TOP_K >= 512 * 8 else 128
    idx_t, gate_t, rank_t, cnt = _router(x, w_router.T, b_router.reshape(n_exp, 1))
    counts = cnt[:, 0].astype(i32)
    padded = (counts + tb - 1) // tb * tb
    pad_end = jnp.cumsum(padded)
    pad_start = pad_end - padded
    pos = pad_start[idx_t] + rank_t
    n_blk = -(-(r * TOP_K) // tb) + n_exp
    tok = jnp.broadcast_to(jnp.arange(r, dtype=i32)[None, :], (TOP_K, r))
    row_tok = jnp.zeros((n_blk * tb,), i32).at[pos.reshape(-1)].set(tok.reshape(-1), unique_indices=True)
    blk_start = jnp.arange(n_blk, dtype=i32) * tb
    block_e = jnp.minimum(jnp.searchsorted(pad_end, blk_start, side='right'), n_exp - 1).astype(i32)
    n_used = (pad_end[-1:] // tb).astype(i32)
    xs = _dispatch(x, row_tok, tb)
    ys = _experts(xs, block_e, n_used, w_gu, b_gu, w_down, b_down, tb)
    return _combine_ln(ys, pos, gate_t.T, x, g, b, alpha=alpha)


def _rope_tables(pos, rope):
    def cos_sin(d):
        inv = ROPE_THETA ** (-jnp.arange(0, d, 2, dtype=f32) / d)
        ang = pos.astype(f32)[:, None] * inv[None, :]
        return jnp.cos(ang), jnp.sin(ang)
    c, s = cos_sin(rope)
    mc = jnp.concatenate([c, c] * (LANES // rope), -1)
    ms = jnp.concatenate([-s, s] * (LANES // rope), -1)
    c, s = cos_sin(DIFF_ROT)
    n = pos.shape[0]
    dc = jnp.concatenate([c, c, jnp.ones((n, LANES - DIFF_ROT), f32)], -1)
    ds = jnp.concatenate([-s, s, jnp.zeros((n, LANES - DIFF_ROT), f32)], -1)
    return mc, ms, dc, ds


def kernel(x_prompt, x_sample, cache_mla_ckv, cache_mla_kpe, cache_diff_k, cache_diff_v, state_gla, meta_tokens, w_in_attn, mla_q_norm, mla_w_uq, mla_kv_norm, mla_w_uk, mla_w_uv, diff_lambda, diff_subln, w_out_attn, w_in_gla, gla_w_gate, gla_b_gate, gla_norm, w_out_gla, ln1_g, ln1_b, ln2_g, ln2_b, moe_w_router, moe_b_router, moe_w_gu, moe_b_gu, moe_w_down, moe_b_down):
    bp, seq, d = x_prompt.shape
    bs, ts, _ = x_sample.shape
    n_meta = meta_tokens.shape[0]
    past_len = cache_mla_ckv.shape[2]
    depth = ln1_g.shape[0]
    assert depth == 2 and ts == n_meta and seq % CHUNK == 0
    alpha = (2 * depth) ** 0.25

    n_heads = mla_w_uq.shape[2]
    q_lora = mla_w_uq.shape[1]
    kv_lora = mla_w_uk.shape[1]
    nope = mla_w_uk.shape[3]
    rope = mla_w_uq.shape[3] - nope
    vdim = mla_w_uv.shape[3]
    diff_heads = cache_diff_k.shape[3]
    diff_qk = cache_diff_k.shape[5]
    diff_cols = diff_heads * 2 * diff_qk
    assert nope == LANES and 2 * rope == LANES and diff_qk == LANES and n_heads % 2 == 0
    assert cache_diff_v.shape[4] == 2 * diff_qk and kv_lora % LANES == 0 and q_lora % LANES == 0

    n_f = bp * seq
    row_meta = n_f
    row_samp = n_f + bp * n_meta
    n_rows = row_samp + bs * ts
    x = jnp.concatenate([x_prompt.reshape(n_f, d),
                         jnp.broadcast_to(meta_tokens.astype(f32)[None], (bp, n_meta, d)).reshape(bp * n_meta, d),
                         x_sample.reshape(bs * ts, d)], 0)
    pos = jnp.concatenate([jnp.tile(n_meta + jnp.arange(seq, dtype=i32), bp),
                           jnp.tile(jnp.arange(n_meta, dtype=i32), bp),
                           jnp.tile(past_len + jnp.arange(ts, dtype=i32), bs)])
    tq = 256 if seq % 256 == 0 and seq >= 512 else CHUNK * (2 if seq % (2 * CHUNK) == 0 else 1)

    w_in = w_in_attn[0]
    s0, s1, s2, s3, s4 = q_lora, q_lora + kv_lora, q_lora + kv_lora + rope, q_lora + kv_lora + rope + diff_cols, \
        q_lora + kv_lora + rope + 2 * diff_cols
    w_in_perm = jnp.concatenate([w_in[:, :s1], w_in[:, s2:], w_in[:, s1:s2],
                                 jnp.zeros((d, LANES - rope), f32)], 1).astype(bf16)
    y = _matmul(x, w_in_perm, f32, "attn_in_proj")
    wuq = mla_w_uq[0]
    wuq_perm = jnp.concatenate([wuq[:, :, :nope].reshape(q_lora, n_heads * nope),
                                wuq[:, :, nope:].reshape(q_lora, n_heads * rope)], 1).astype(bf16)
    wuk_t = jnp.transpose(mla_w_uk[0], (1, 2, 0)).astype(bf16)
    wuv = jnp.transpose(mla_w_uv[0], (1, 0, 2)).astype(bf16)
    tabs = _rope_tables(pos, rope)
    qm, ckv, kpe, km, dqb, dk, dkb, dv, dvb = _attn_prep(
        y, tabs, mla_q_norm[0][None], mla_kv_norm[0][None], wuq_perm, wuk_t,
        n_heads=n_heads, q_lora=q_lora, kv_lora=kv_lora, nope=nope, rope=rope, diff_cols=diff_cols)

    mla_scale = (nope + rope) ** -0.5
    diff_scale = diff_qk ** -0.5
    lambda_init = 0.8 - 0.6 * math.exp(-0.3 * 0)
    lam_vecs = diff_lambda[0].astype(f32)
    subln = diff_subln[0][None]
    common_d = dict(n_heads=diff_heads, scale=diff_scale, lambda_init=lambda_init)
    o_mla = jnp.concatenate([
        _mla_flash(qm, km, wuv, n_batch=bp, seq=seq, n_meta=n_meta, scale=mla_scale, kv_lora=kv_lora, tq=tq),
        _mla_small(qm, km, wuv, None, n_seq=bp, tq=n_meta, row0=row_meta, scale=mla_scale, kv_lora=kv_lora),
        _mla_small(qm, km, wuv, (cache_mla_ckv[0], cache_mla_kpe[0]), n_seq=bs, tq=ts, row0=row_samp,
                   scale=mla_scale, kv_lora=kv_lora)], 0)
    past_d = (cache_diff_k[0].reshape(bs, past_len, diff_cols), cache_diff_v[0].reshape(bs, past_len, diff_cols))
    o_dif = jnp.concatenate([
        _diff_flash(dqb, dkb, dvb, lam_vecs, subln, n_batch=bp, seq=seq, n_meta=n_meta, tq=tq, **common_d),
        _diff_small(dqb, dkb, dvb, lam_vecs, subln, None, n_seq=bp, tq=n_meta, row0=row_meta, **common_d),
        _diff_small(dqb, dkb, dvb, lam_vecs, subln, past_d, n_seq=bs, tq=ts, row0=row_samp, **common_d)], 0)
    w_out = w_out_attn[0].astype(bf16)
    mla_out = n_heads * vdim
    x = _proj_ln([o_mla, o_dif], [w_out[:mla_out], w_out[mla_out:]], x, ln1_g[0][None], ln1_b[0][None], alpha=alpha)
    x = _moe_ln(x, moe_w_router[0], moe_b_router[0], moe_w_gu[0], moe_b_gu[0], moe_w_down[0], moe_b_down[0],
                ln2_g[0][None], ln2_b[0][None], alpha=alpha)

    gla_heads = state_gla.shape[2]
    gla_dk, gla_dv = state_gla.shape[3], state_gla.shape[4]
    qk_cols, v_cols = gla_heads * gla_dk, gla_heads * gla_dv
    rank = gla_w_gate.shape[1]
    w_gla = jnp.concatenate([w_in_gla[0], jnp.zeros((d, LANES - rank), f32)], 1).astype(bf16)
    yg = _matmul(x, w_gla, f32, "gla_in_proj")
    w_gate = jnp.concatenate([gla_w_gate[0], jnp.zeros((LANES - rank, qk_cols), f32)], 0)
    b_gate = gla_b_gate[0][None]
    g_norm = gla_norm[0][None]
    gla_kw = dict(n_heads=gla_heads, qk_cols=qk_cols, v_cols=v_cols)
    s_init = jnp.concatenate([jnp.zeros((bp,) + state_gla.shape[2:], f32), state_gla[0].astype(f32)], 0)
    o_short, s_short = _gla(yg, w_gate, b_gate, g_norm, s_init, n_seq=bp + bs, n_chunks=1, chunk=n_meta,
                            row0=row_meta, **gla_kw)
    o_frames, s_frames = _gla(yg, w_gate, b_gate, g_norm, s_short[:bp], n_seq=bp, n_chunks=seq // GLA_CHUNK,
                              chunk=GLA_CHUNK, row0=0, **gla_kw)
    o_gla = jnp.concatenate([o_frames, o_short], 0)
    x = _proj_ln([o_gla], [w_out_gla[0].astype(bf16)], x, ln1_g[1][None], ln1_b[1][None], alpha=alpha)
    x = _moe_ln(x, moe_w_router[1], moe_b_router[1], moe_w_gu[1], moe_b_gu[1], moe_w_down[1], moe_b_down[1],
                ln2_g[1][None], ln2_b[1][None], alpha=alpha)

    def rows_p(a, tail):
        c = a.shape[1]
        return jnp.concatenate([a[row_meta:row_samp].reshape(bp, n_meta, c), a[:n_f].reshape(bp, seq, c)],
                               1).reshape((1, bp, n_meta + seq) + tail)

    def rows_s(a, tail):
        return a[row_samp:].reshape((1, bs, ts) + tail)

    kpe = kpe[:, :rope]
    t_dk, t_dv = (diff_heads, 2, diff_qk), (diff_heads, 2 * diff_qk)
    return (x[:n_f].reshape(bp, seq, d), x[row_samp:].reshape(bs, ts, d),
            rows_p(ckv, (kv_lora,)), rows_p(kpe, (rope,)), rows_p(dk, t_dk), rows_p(dv, t_dv), s_frames[None],
            rows_s(ckv, (kv_lora,)), rows_s(kpe, (rope,)), rows_s(dk, t_dk), rows_s(dv, t_dv), s_short[bp:][None])
```

```python
import functools
import math

import jax
import jax.numpy as jnp
from jax import lax
from jax.experimental import pallas as pl
from jax.experimental.pallas import tpu as pltpu

f32 = jnp.float32
bf16 = jnp.bfloat16
i32 = jnp.int32

CHUNK = 64
ROPE_THETA = 500000.0
GLA_TAU = 16.0
GLA_CHUNK = 64
TOP_K = 4
SWIGLU_LIMIT = 7.0
SWIGLU_ALPHA = 1.702
DIFF_ROT = 32
NEG_BIG = -1e30
ROW_DMA_UNROLL = 8

LANES = 128
VMEM_LIMIT_BYTES = 56 * 1024 * 1024


def _params(*sem):
    return pltpu.CompilerParams(dimension_semantics=sem, vmem_limit_bytes=VMEM_LIMIT_BYTES)


def _tile(n, prefs):
    for t in prefs:
        if n % t == 0:
            return t
    raise ValueError(f"no tile in {prefs} divides {n}")


def _dot(a, b, precision=None):
    return jnp.dot(a, b, preferred_element_type=f32, precision=precision)


def _dot_nt(a, b, precision=None):
    return lax.dot_general(a, b, (((1,), (1,)), ((), ())), preferred_element_type=f32, precision=precision)


def _rms(x, g, eps=1e-6):
    return x * lax.rsqrt(jnp.mean(x * x, -1, keepdims=True) + eps) * g


def _layer_norm(x, g, b, eps=1e-5):
    mu = jnp.mean(x, -1, keepdims=True)
    xc = x - mu
    var = jnp.mean(xc * xc, -1, keepdims=True)
    return xc * lax.rsqrt(var + eps) * g + b


def _store_token_major(ref, lead, value):
    tm, d = value.shape
    groups = d // LANES
    for s in range(groups):
        ref[lead + (pl.ds(s, tm, stride=groups), slice(None))] = value[:, s * LANES:(s + 1) * LANES]


def _load_token_major(ref, lead, tm, groups, s):
    return ref[lead + (pl.ds(s, tm, stride=groups), slice(None))]


def _mm_body(x_ref, w_ref, o_ref):
    o_ref[...] = _dot(x_ref[...].astype(bf16), w_ref[...].astype(bf16)).astype(o_ref.dtype)


def _matmul(x, w, out_dtype, name):
    m, k = x.shape
    n = w.shape[1]
    tm = _tile(m, (512, 256, 128, 64, 32, 16, 8))
    tn = _tile(n, (1408, 1024, 896, 768, 640, 512, 384, 256, 128))
    return pl.pallas_call(
        _mm_body,
        grid=(m // tm, n // tn),
        in_specs=[pl.BlockSpec((tm, k), lambda i, j: (i, 0)),
                  pl.BlockSpec((k, tn), lambda i, j: (0, j))],
        out_specs=pl.BlockSpec((tm, tn), lambda i, j: (i, j)),
        out_shape=jax.ShapeDtypeStruct((m, n), out_dtype),
        compiler_params=_params("parallel", "parallel"),
        name=name,
    )(x, w)


def _attn_prep_body(y_ref, mc_ref, ms_ref, dc_ref, ds_ref, qn_ref, kvn_ref, wuq_ref, wuk_ref,
                    qm_ref, ckv_ref, kpe_ref, km_ref, dqb_ref, dk_ref, dkb_ref, dv_ref, dvb_ref,
                    *, n_heads, q_lora, kv_lora, nope, rope, diff_cols):
    tm = y_ref.shape[0]
    lane = lax.broadcasted_iota(i32, (tm, LANES), 1)
    mc, ms, dc, ds = mc_ref[...], ms_ref[...], dc_ref[...], ds_ref[...]
    half = rope // 2

    def rope_pair(v):
        rolled = jnp.where((lane % rope) < half, pltpu.roll(v, LANES - half, 1), pltpu.roll(v, half, 1))
        return v * mc + rolled * ms

    dhalf = DIFF_ROT // 2

    def rope_diff(v):
        rolled = jnp.where(lane < dhalf, pltpu.roll(v, LANES - dhalf, 1), pltpu.roll(v, dhalf, 1))
        return v * dc + rolled * ds

    qn = _rms(y_ref[:, 0:q_lora], qn_ref[...]).astype(bf16)
    q = _dot(qn, wuq_ref[...])
    for h in range(n_heads):
        lat = _dot(q[:, h * nope:(h + 1) * nope].astype(bf16), wuk_ref[h])
        qm_ref[h, :, 0:kv_lora] = lat.astype(bf16)
    base = n_heads * nope
    for g in range(n_heads // 2):
        pe = rope_pair(q[:, base + g * LANES: base + (g + 1) * LANES]).astype(bf16)
        qm_ref[2 * g, :, kv_lora:kv_lora + rope] = pe[:, 0:rope]
        qm_ref[2 * g + 1, :, kv_lora:kv_lora + rope] = pe[:, rope:2 * rope]
    ckv = _rms(y_ref[:, q_lora:q_lora + kv_lora], kvn_ref[...])
    ckv_ref[...] = ckv
    km_ref[:, 0:kv_lora] = ckv.astype(bf16)
    o_dq = q_lora + kv_lora
    o_dk = o_dq + diff_cols
    o_dv = o_dk + diff_cols
    o_pe = o_dv + diff_cols
    kpe = rope_pair(y_ref[:, o_pe:o_pe + LANES])
    kpe_ref[...] = kpe
    km_ref[:, kv_lora:kv_lora + rope] = kpe[:, 0:rope].astype(bf16)
    for g in range(diff_cols // LANES):
        dq = rope_diff(y_ref[:, o_dq + g * LANES:o_dq + (g + 1) * LANES])
        dqb_ref[g] = dq.astype(bf16)
        dk = rope_diff(y_ref[:, o_dk + g * LANES:o_dk + (g + 1) * LANES])
        dk_ref[:, g * LANES:(g + 1) * LANES] = dk
        dkb_ref[g] = dk.astype(bf16)
    dv = y_ref[:, o_dv:o_dv + diff_cols]
    dv_ref[...] = dv
    vw = dvb_ref.shape[2]
    for h in range(dvb_ref.shape[0]):
        dvb_ref[h] = dv[:, h * vw:(h + 1) * vw].astype(bf16)


def _attn_prep(y, tabs, q_norm, kv_norm, wuq, wukT, *, n_heads, q_lora, kv_lora, nope, rope, diff_cols,
               diff_heads):
    r, ycols = y.shape
    tm = _tile(r, (256, 128, 64, 32, 16))
    dm = kv_lora + rope
    n_qk = diff_cols // LANES
    vw = diff_cols // diff_heads
    row = lambda i: (i, 0)
    mid = lambda i: (0, i, 0)
    const2 = lambda i: (0, 0)
    body = functools.partial(_attn_prep_body, n_heads=n_heads, q_lora=q_lora, kv_lora=kv_lora,
                             nope=nope, rope=rope, diff_cols=diff_cols)
    return pl.pallas_call(
        body,
        grid=(r // tm,),
        in_specs=[pl.BlockSpec((tm, ycols), row)]
                 + [pl.BlockSpec((tm, LANES), row)] * 4
                 + [pl.BlockSpec((1, q_lora), const2), pl.BlockSpec((1, kv_lora), const2),
                    pl.BlockSpec(wuq.shape, const2), pl.BlockSpec(wukT.shape, lambda i: (0, 0, 0))],
        out_specs=[pl.BlockSpec((n_heads, tm, dm), mid),
                   pl.BlockSpec((tm, kv_lora), row), pl.BlockSpec((tm, LANES), row),
                   pl.BlockSpec((tm, dm), row), pl.BlockSpec((n_qk, tm, LANES), mid),
                   pl.BlockSpec((tm, diff_cols), row), pl.BlockSpec((n_qk, tm, LANES), mid),
                   pl.BlockSpec((tm, diff_cols), row), pl.BlockSpec((diff_heads, tm, vw), mid)],
        out_shape=[jax.ShapeDtypeStruct((n_heads, r, dm), bf16),
                   jax.ShapeDtypeStruct((r, kv_lora), f32), jax.ShapeDtypeStruct((r, LANES), f32),
                   jax.ShapeDtypeStruct((r, dm), bf16), jax.ShapeDtypeStruct((n_qk, r, LANES), bf16),
                   jax.ShapeDtypeStruct((r, diff_cols), f32), jax.ShapeDtypeStruct((n_qk, r, LANES), bf16),
                   jax.ShapeDtypeStruct((r, diff_cols), f32), jax.ShapeDtypeStruct((diff_heads, r, vw), bf16)],
        compiler_params=_params("parallel"),
        name="attn_prep",
    )(y, *tabs, q_norm, kv_norm, wuq, wukT)


def _chunk_mask(tq, tk):
    rows = lax.broadcasted_iota(i32, (tq, tk), 0)
    cols = lax.broadcasted_iota(i32, (tq, tk), 1)
    return (cols // CHUNK) <= (rows // CHUNK)


def _softmax_step(s, v, m_ref, l_ref, acc_ref, c):
    m_prev = m_ref[c]
    m_new = jnp.maximum(m_prev, jnp.max(s, -1, keepdims=True))
    a = jnp.exp(m_prev - m_new)
    p = jnp.exp(s - m_new)
    l_ref[c] = a * l_ref[c] + jnp.sum(p, -1, keepdims=True)
    acc_ref[c] = a * acc_ref[c] + _dot(p.astype(bf16), v)
    m_ref[c] = m_new


def _init_softmax(m_ref, l_ref, acc_ref):
    m_ref[...] = jnp.full(m_ref.shape, NEG_BIG, f32)
    l_ref[...] = jnp.zeros(l_ref.shape, f32)
    acc_ref[...] = jnp.zeros(acc_ref.shape, f32)


def _mla_flash_body(q_ref, kf_ref, kmeta_ref, wuv_ref, o_ref, m_ref, l_ref, acc_ref, *, scale, kv_lora):
    n_heads, tq, _ = q_ref.shape
    tk = kf_ref.shape[0]
    vdim = wuv_ref.shape[2]
    qi, kj = pl.program_id(1), pl.program_id(2)

    def attend(k_ref, masked):
        mask = _chunk_mask(tq, tk) if masked else None

        def head(h, carry):
            k = k_ref[...]
            s = _dot_nt(q_ref[h], k) * scale
            if masked:
                s = jnp.where(mask, s, NEG_BIG)
            _softmax_step(s, k[:, 0:kv_lora], m_ref, l_ref, acc_ref, h)
            return carry

        lax.fori_loop(0, n_heads, head, 0, unroll=2)

    @pl.when(kj == 0)
    def _():
        _init_softmax(m_ref, l_ref, acc_ref)
        attend(kmeta_ref, False)

    @pl.when(kj < qi)
    def _():
        attend(kf_ref, False)

    @pl.when(kj == qi)
    def _():
        attend(kf_ref, True)

    @pl.when(kj == pl.num_programs(2) - 1)
    def _():
        for h in range(n_heads):
            o = (acc_ref[h] / l_ref[h]).astype(bf16)
            o_ref[:, h * vdim:(h + 1) * vdim] = _dot(o, wuv_ref[h]).astype(o_ref.dtype)


def _mla_flash(qm, km, wuv, *, n_batch, seq, n_meta, scale, kv_lora, tq):
    n_heads, r, dm = qm.shape
    vdim = wuv.shape[2]
    nq = seq // tq
    meta_blk0 = n_batch * seq // n_meta
    body = functools.partial(_mla_flash_body, scale=scale, kv_lora=kv_lora)
    return pl.pallas_call(
        body,
        grid=(n_batch, nq, nq),
        in_specs=[pl.BlockSpec((n_heads, tq, dm), lambda b, i, j: (0, b * nq + i, 0)),
                  pl.BlockSpec((tq, dm), lambda b, i, j: (b * nq + jnp.minimum(i, j), 0)),
                  pl.BlockSpec((n_meta, dm), lambda b, i, j: (meta_blk0 + b, 0)),
                  pl.BlockSpec(wuv.shape, lambda b, i, j: (0, 0, 0))],
        out_specs=pl.BlockSpec((tq, n_heads * vdim), lambda b, i, j: (b * nq + i, 0)),
        out_shape=jax.ShapeDtypeStruct((n_batch * seq, n_heads * vdim), bf16),
        scratch_shapes=[pltpu.VMEM((n_heads, tq, 1), f32), pltpu.VMEM((n_heads, tq, 1), f32),
                        pltpu.VMEM((n_heads, tq, kv_lora), f32)],
        compiler_params=_params("parallel", "parallel", "arbitrary"),
        name="mla_flash",
    )(qm, km, km, wuv)


def _diff_lambda(lam_ref, lambda_init):
    lv = lam_ref[...]
    return (jnp.exp(jnp.sum(lv[0:1] * lv[1:2], keepdims=True))
            - jnp.exp(jnp.sum(lv[2:3] * lv[3:4], keepdims=True)) + lambda_init)


def _diff_flash_body(q_ref, kf_ref, vf_ref, kmeta_ref, vmeta_ref, lam_ref, sub_ref, o_ref,
                     m_ref, l_ref, acc_ref, *, scale, lambda_init):
    n_qk, tq, _ = q_ref.shape
    tk = kf_ref.shape[1]
    vw = vf_ref.shape[2]
    qi, kj = pl.program_id(1), pl.program_id(2)

    def attend(k_ref, v_ref, masked):
        mask = _chunk_mask(tq, tk) if masked else None

        def part(c, carry):
            s = _dot_nt(q_ref[c], k_ref[c]) * scale
            if masked:
                s = jnp.where(mask, s, NEG_BIG)
            _softmax_step(s, v_ref[c // 2], m_ref, l_ref, acc_ref, c)
            return carry

        lax.fori_loop(0, n_qk, part, 0, unroll=2)

    @pl.when(kj == 0)
    def _():
        _init_softmax(m_ref, l_ref, acc_ref)
        attend(kmeta_ref, vmeta_ref, False)

    @pl.when(kj < qi)
    def _():
        attend(kf_ref, vf_ref, False)

    @pl.when(kj == qi)
    def _():
        attend(kf_ref, vf_ref, True)

    @pl.when(kj == pl.num_programs(2) - 1)
    def _():
        lam = _diff_lambda(lam_ref, lambda_init)
        for h in range(n_qk // 2):
            o = acc_ref[2 * h] / l_ref[2 * h] - lam * (acc_ref[2 * h + 1] / l_ref[2 * h + 1])
            o_ref[:, h * vw:(h + 1) * vw] = (_rms(o, sub_ref[...]) * (1.0 - lambda_init)).astype(o_ref.dtype)


def _diff_flash(dqb, dkb, dvb, lam_vecs, subln, *, n_batch, seq, n_meta, scale, lambda_init, tq):
    n_qk, r, d = dqb.shape
    n_heads, _, vw = dvb.shape
    nq = seq // tq
    meta_blk0 = n_batch * seq // n_meta
    body = functools.partial(_diff_flash_body, scale=scale, lambda_init=lambda_init)
    qmap = lambda b, i, j: (0, b * nq + i, 0)
    kmap = lambda b, i, j: (0, b * nq + jnp.minimum(i, j), 0)
    mmap = lambda b, i, j: (0, meta_blk0 + b, 0)
    const2 = lambda b, i, j: (0, 0)
    return pl.pallas_call(
        body,
        grid=(n_batch, nq, nq),
        in_specs=[pl.BlockSpec((n_qk, tq, d), qmap), pl.BlockSpec((n_qk, tq, d), kmap),
                  pl.BlockSpec((n_heads, tq, vw), kmap),
                  pl.BlockSpec((n_qk, n_meta, d), mmap), pl.BlockSpec((n_heads, n_meta, vw), mmap),
                  pl.BlockSpec(lam_vecs.shape, const2), pl.BlockSpec(subln.shape, const2)],
        out_specs=pl.BlockSpec((tq, n_heads * vw), lambda b, i, j: (b * nq + i, 0)),
        out_shape=jax.ShapeDtypeStruct((n_batch * seq, n_heads * vw), bf16),
        scratch_shapes=[pltpu.VMEM((n_qk, tq, 1), f32), pltpu.VMEM((n_qk, tq, 1), f32),
                        pltpu.VMEM((n_qk, tq, vw), f32)],
        compiler_params=_params("parallel", "parallel", "arbitrary"),
        name="diff_flash",
    )(dqb, dkb, dvb, dkb, dvb, lam_vecs, subln)


def _mla_small_body(*refs, scale, kv_lora, has_past):
    if has_past:
        q_ref, kown_ref, ckvp_ref, kpep_ref, wuv_ref, o_ref = refs
    else:
        q_ref, kown_ref, wuv_ref, o_ref = refs
    n_heads, tq, dm = q_ref.shape
    vdim = wuv_ref.shape[2]
    q = q_ref[...].reshape(n_heads * tq, dm)
    kown = kown_ref[...]
    s_own = _dot_nt(q, kown) * scale
    m = jnp.max(s_own, -1, keepdims=True)
    if has_past:
        kp = ckvp_ref[0].astype(bf16)
        s_past = (_dot_nt(q[:, 0:kv_lora], kp) + _dot_nt(q[:, kv_lora:dm], kpep_ref[0].astype(bf16))) * scale
        m = jnp.maximum(m, jnp.max(s_past, -1, keepdims=True))
    p_own = jnp.exp(s_own - m)
    l = jnp.sum(p_own, -1, keepdims=True)
    acc = _dot(p_own.astype(bf16), kown[:, 0:kv_lora])
    if has_past:
        p_past = jnp.exp(s_past - m)
        l = l + jnp.sum(p_past, -1, keepdims=True)
        acc = acc + _dot(p_past.astype(bf16), kp)
    o = (acc / l).astype(bf16)
    for h in range(n_heads):
        o_ref[:, h * vdim:(h + 1) * vdim] = _dot(o[h * tq:(h + 1) * tq], wuv_ref[h]).astype(o_ref.dtype)


def _mla_small(qm, km, wuv, past, *, n_seq, tq, row0, scale, kv_lora):
    n_heads, r, dm = qm.shape
    vdim = wuv.shape[2]
    blk0 = row0 // tq
    has_past = past is not None
    in_specs = [pl.BlockSpec((n_heads, tq, dm), lambda b: (0, blk0 + b, 0)),
                pl.BlockSpec((tq, dm), lambda b: (blk0 + b, 0))]
    args = [qm, km]
    if has_past:
        ckvp, kpep = past
        in_specs += [pl.BlockSpec((1,) + ckvp.shape[1:], lambda b: (b, 0, 0)),
                     pl.BlockSpec((1,) + kpep.shape[1:], lambda b: (b, 0, 0))]
        args += [ckvp, kpep]
    in_specs.append(pl.BlockSpec(wuv.shape, lambda b: (0, 0, 0)))
    args.append(wuv)
    body = functools.partial(_mla_small_body, scale=scale, kv_lora=kv_lora, has_past=has_past)
    return pl.pallas_call(
        body,
        grid=(n_seq,),
        in_specs=in_specs,
        out_specs=pl.BlockSpec((tq, n_heads * vdim), lambda b: (b, 0)),
        out_shape=jax.ShapeDtypeStruct((n_seq * tq, n_heads * vdim), bf16),
        compiler_params=_params("parallel"),
        name="mla_small_past" if has_past else "mla_small",
    )(*args)


def _diff_small_body(*refs, scale, lambda_init, has_past):
    if has_past:
        q_ref, kown_ref, vown_ref, kp_ref, vp_ref, lam_ref, sub_ref, o_ref = refs
    else:
        q_ref, kown_ref, vown_ref, lam_ref, sub_ref, o_ref = refs
    d = q_ref.shape[2]
    outs = []
    for i in range(2):
        q = q_ref[i]
        s_own = _dot_nt(q, kown_ref[i]) * scale
        m = jnp.max(s_own, -1, keepdims=True)
        if has_past:
            s_past = _dot_nt(q, kp_ref[0, :, i * d:(i + 1) * d].astype(bf16)) * scale
            m = jnp.maximum(m, jnp.max(s_past, -1, keepdims=True))
        p_own = jnp.exp(s_own - m)
        l = jnp.sum(p_own, -1, keepdims=True)
        acc = _dot(p_own.astype(bf16), vown_ref[0])
        if has_past:
            p_past = jnp.exp(s_past - m)
            l = l + jnp.sum(p_past, -1, keepdims=True)
            acc = acc + _dot(p_past.astype(bf16), vp_ref[0].astype(bf16))
        outs.append(acc / l)
    o = outs[0] - _diff_lambda(lam_ref, lambda_init) * outs[1]
    o_ref[...] = (_rms(o, sub_ref[...]) * (1.0 - lambda_init)).astype(o_ref.dtype)


def _diff_small(dqb, dkb, dvb, lam_vecs, subln, past, *, n_seq, tq, row0, scale, lambda_init):
    n_qk, r, d = dqb.shape
    n_heads, _, vw = dvb.shape
    blk0 = row0 // tq
    has_past = past is not None
    own = lambda b, h: (h, blk0 + b, 0)
    in_specs = [pl.BlockSpec((2, tq, d), own), pl.BlockSpec((2, tq, d), own), pl.BlockSpec((1, tq, vw), own)]
    args = [dqb, dkb, dvb]
    if has_past:
        kp, vp = past
        in_specs += [pl.BlockSpec((1, kp.shape[1], vw), lambda b, h: (b, 0, h)),
                     pl.BlockSpec((1, vp.shape[1], vw), lambda b, h: (b, 0, h))]
        args += [kp, vp]
    in_specs += [pl.BlockSpec(lam_vecs.shape, lambda b, h: (0, 0)), pl.BlockSpec(subln.shape, lambda b, h: (0, 0))]
    args += [lam_vecs, subln]
    body = functools.partial(_diff_small_body, scale=scale, lambda_init=lambda_init, has_past=has_past)
    return pl.pallas_call(
        body,
        grid=(n_seq, n_heads),
        in_specs=in_specs,
        out_specs=pl.BlockSpec((tq, vw), lambda b, h: (b, h)),
        out_shape=jax.ShapeDtypeStruct((n_seq * tq, n_heads * vw), bf16),
        compiler_params=_params("parallel", "parallel"),
        name="diff_small_past" if has_past else "diff_small",
    )(*args)


def _gla_body(q_ref, k_ref, v_ref, r_ref, g_ref, wg_ref, bg_ref, gn_ref, s0_ref, o_ref, sout_ref, s_ref,
              *, n_heads):
    c = q_ref.shape[0]
    dk = q_ref.shape[1] // n_heads
    dv = v_ref.shape[1] // n_heads
    step = pl.program_id(1)

    @pl.when(step == 0)
    def _():
        s_ref[...] = s0_ref[0]

    pre = _dot(g_ref[...], wg_ref[...], precision=lax.Precision.HIGHEST) + bg_ref[...]
    log_a = jax.nn.log_sigmoid(pre) / GLA_TAU
    rows = lax.broadcasted_iota(i32, (c, c), 0)
    cols = lax.broadcasted_iota(i32, (c, c), 1)
    causal = rows >= cols
    tril = causal.astype(f32)
    for h in range(n_heads):
        la = log_a[:, h * dk:(h + 1) * dk]
        b = _dot(tril, la, precision=lax.Precision.HIGHEST)
        b_t = b.T
        b_last = b_t[:, c - 1:c]
        qh = q_ref[:, h * dk:(h + 1) * dk] * (dk ** -0.5)
        kh = k_ref[:, h * dk:(h + 1) * dk]
        vh = v_ref[:, h * dv:(h + 1) * dv].astype(bf16)
        q_in = (qh * jnp.exp(b)).astype(bf16)
        k_out = (kh * jnp.exp(-b)).astype(bf16)
        scores = jnp.where(causal, _dot_nt(q_in, k_out), 0.0)
        s_prev = s_ref[h]
        o = _dot(scores.astype(bf16), vh) + _dot(q_in, s_prev.astype(bf16))
        k_dec_t = (kh.T * jnp.exp(b_last - b_t)).astype(bf16)
        s_ref[h] = jnp.exp(b_last) * s_prev + _dot(k_dec_t, vh)
        gate = jax.nn.silu(r_ref[:, h * dv:(h + 1) * dv])
        o_ref[:, h * dv:(h + 1) * dv] = (_rms(o, gn_ref[...]) * gate).astype(o_ref.dtype)

    @pl.when(step == pl.num_programs(1) - 1)
    def _():
        sout_ref[0] = s_ref[...]


def _gla(y, w_gate, b_gate, g_norm, s0, *, n_seq, n_chunks, chunk, row0, n_heads, qk_cols, v_cols):
    blk0 = row0 // chunk
    rmap = lambda col: (lambda b, s: (blk0 + b * n_chunks + s, col))
    g_col = (2 * qk_cols + 2 * v_cols) // LANES
    const2 = lambda b, s: (0, 0)
    smap = lambda b, s: (b, 0, 0, 0)
    body = functools.partial(_gla_body, n_heads=n_heads)
    return pl.pallas_call(
        body,
        grid=(n_seq, n_chunks),
        in_specs=[pl.BlockSpec((chunk, qk_cols), rmap(0)), pl.BlockSpec((chunk, qk_cols), rmap(1)),
                  pl.BlockSpec((chunk, v_cols), rmap(2 * qk_cols // v_cols)),
                  pl.BlockSpec((chunk, v_cols), rmap(2 * qk_cols // v_cols + 1)),
                  pl.BlockSpec((chunk, LANES), rmap(g_col)),
                  pl.BlockSpec(w_gate.shape, const2), pl.BlockSpec(b_gate.shape, const2),
                  pl.BlockSpec(g_norm.shape, const2),
                  pl.BlockSpec((1,) + s0.shape[1:], smap)],
        out_specs=[pl.BlockSpec((chunk, v_cols), lambda b, s: (b * n_chunks + s, 0)),
                   pl.BlockSpec((1,) + s0.shape[1:], smap)],
        out_shape=[jax.ShapeDtypeStruct((n_seq * n_chunks * chunk, v_cols), bf16),
                   jax.ShapeDtypeStruct(s0.shape, f32)],
        scratch_shapes=[pltpu.VMEM(s0.shape[1:], f32)],
        compiler_params=_params("parallel", "arbitrary"),
        name=f"gla_chunk{chunk}",
    )(y, y, y, y, y, w_gate, b_gate, g_norm, s0)


def _proj_ln_body(*refs, n_in, alpha):
    a_refs = refs[:n_in]
    w_refs = refs[n_in:2 * n_in]
    x_ref, g_ref, b_ref, o_ref, ot_ref = refs[2 * n_in:]
    h = _dot(a_refs[0][...], w_refs[0][...])
    for a_ref, w_ref in zip(a_refs[1:], w_refs[1:]):
        h = h + _dot(a_ref[...], w_ref[...])
    out = _layer_norm(alpha * x_ref[...] + h, g_ref[...], b_ref[...])
    o_ref[...] = out
    _store_token_major(ot_ref, (), out)


def _proj_ln(acts, weights, x, g, b, *, alpha):
    r, d = x.shape
    groups = d // LANES
    tm = _tile(r, (256, 128, 64, 32, 16, 8))
    row = lambda i: (i, 0)
    const2 = lambda i: (0, 0)
    body = functools.partial(_proj_ln_body, n_in=len(acts), alpha=alpha)
    return pl.pallas_call(
        body,
        grid=(r // tm,),
        in_specs=[pl.BlockSpec((tm, a.shape[1]), row) for a in acts]
                 + [pl.BlockSpec(w.shape, const2) for w in weights]
                 + [pl.BlockSpec((tm, d), row), pl.BlockSpec((1, d), const2), pl.BlockSpec((1, d), const2)],
        out_specs=[pl.BlockSpec((tm, d), row), pl.BlockSpec((tm * groups, LANES), row)],
        out_shape=[jax.ShapeDtypeStruct((r, d), f32), jax.ShapeDtypeStruct((r * groups, LANES), f32)],
        compiler_params=_params("parallel"),
        name="proj_ln",
    )(*acts, *weights, x, g, b)


def _router_body(x_ref, w_ref, b_ref, idx_ref, gate_ref, rank_ref, cnt_ref, base_ref):
    n_exp = w_ref.shape[0]
    tm = x_ref.shape[0]

    @pl.when(pl.program_id(0) == 0)
    def _():
        base_ref[...] = jnp.zeros(base_ref.shape, f32)

    logits = _dot_nt(w_ref[...], x_ref[...], precision=lax.Precision.HIGHEST) + b_ref[...]
    eid = lax.broadcasted_iota(i32, (n_exp, tm), 0).astype(f32)
    work = logits
    vals, hots = [], []
    for k in range(TOP_K):
        m = jnp.max(work, 0, keepdims=True)
        sel = jnp.min(jnp.where(work == m, eid, float(n_exp)), 0, keepdims=True)
        hot = eid == sel
        idx_ref[k:k + 1, :] = sel.astype(i32)
        vals.append(m)
        hots.append(hot)
        work = jnp.where(hot, -jnp.inf, work)
    exps = [jnp.exp(v - vals[0]) for v in vals]
    denom = functools.reduce(jnp.add, exps)
    for k in range(TOP_K):
        gate_ref[k:k + 1, :] = exps[k] / denom
    hot_all = functools.reduce(jnp.logical_or, hots).astype(bf16)
    t_row = lax.broadcasted_iota(i32, (tm, tm), 0)
    t_col = lax.broadcasted_iota(i32, (tm, tm), 1)
    before = (t_row < t_col).astype(bf16)
    prefix = _dot(hot_all, before) + base_ref[...]
    for k in range(TOP_K):
        rank_ref[k:k + 1, :] = jnp.sum(jnp.where(hots[k], prefix, 0.0), 0, keepdims=True).astype(i32)
    base_ref[...] = base_ref[...] + jnp.sum(hot_all.astype(f32), 1, keepdims=True)
    cnt_ref[...] = jnp.broadcast_to(base_ref[...], cnt_ref.shape)


def _router(x, w_t, b_col):
    r, d = x.shape
    n_exp = w_t.shape[0]
    tm = _tile(r, (512, 256, 128))
    col = lambda i: (0, i)
    const2 = lambda i: (0, 0)
    return pl.pallas_call(
        _router_body,
        grid=(r // tm,),
        in_specs=[pl.BlockSpec((tm, d), lambda i: (i, 0)), pl.BlockSpec((n_exp, d), const2),
                  pl.BlockSpec((n_exp, 1), const2)],
        out_specs=[pl.BlockSpec((TOP_K, tm), col), pl.BlockSpec((TOP_K, tm), col), pl.BlockSpec((TOP_K, tm), col),
                   pl.BlockSpec((n_exp, LANES), const2)],
        out_shape=[jax.ShapeDtypeStruct((TOP_K, r), i32), jax.ShapeDtypeStruct((TOP_K, r), f32),
                   jax.ShapeDtypeStruct((TOP_K, r), i32), jax.ShapeDtypeStruct((n_exp, LANES), f32)],
        scratch_shapes=[pltpu.VMEM((n_exp, 1), f32)],
        compiler_params=_params("arbitrary"),
        name="router",
    )(x, w_t, b_col)


def _expert_body(be_ref, nused_ref, tok_cur_ref, tok_nxt_ref, dst_prev_ref,
                 x_hbm, wg_ref, wu_ref, bg_ref, bu_ref, wd_ref, bd_ref, ys_hbm,
                 gbuf, xb_ref, acc_ref, obuf, gsem, ssem, *, groups):
    i, f = pl.program_id(0), pl.program_id(1)
    n_f = pl.num_programs(1)
    n_used = nused_ref[0]
    tb = xb_ref.shape[0]
    per_step = tb // n_f
    slot = lax.rem(i, 2)
    other = 1 - slot
    used = i < n_used
    has_prev = jnp.logical_and(i >= 1, i <= n_used)

    def gather_row(tok_ref, j, s):
        src = pl.multiple_of(tok_ref[0, 0, j] * groups, groups)
        dst = pl.multiple_of(j * groups, groups)
        return pltpu.make_async_copy(x_hbm.at[pl.ds(src, groups), :], gbuf.at[s, pl.ds(dst, groups), :],
                                     gsem.at[s])

    def scatter_row(j, s):
        src = pl.multiple_of(j * groups, groups)
        dst = pl.multiple_of(dst_prev_ref[0, 0, j] * groups, groups)
        return pltpu.make_async_copy(obuf.at[s, pl.ds(src, groups), :], ys_hbm.at[pl.ds(dst, groups), :],
                                     ssem.at[s])

    def rows(first, count, fn):
        def step(j, carry):
            fn(first + j)
            return carry
        lax.fori_loop(0, count, step, 0, unroll=ROW_DMA_UNROLL)

    @pl.when(jnp.logical_and(i == 0, f == 0))
    def _():
        rows(0, tb, lambda j: gather_row(tok_cur_ref, j, 0).start())

    @pl.when(jnp.logical_and(used, f == 0))
    def _():
        rows(0, tb, lambda j: gather_row(tok_cur_ref, j, slot).wait())
        for s in range(groups):
            xb_ref[:, s * LANES:(s + 1) * LANES] = _load_token_major(gbuf, (slot,), tb, groups, s).astype(bf16)

    @pl.when(used)
    def _():
        x = xb_ref[...]
        gate = jnp.minimum(_dot(x, wg_ref[...].astype(bf16)) + bg_ref[...], SWIGLU_LIMIT)
        up = jnp.clip(_dot(x, wu_ref[...].astype(bf16)) + bu_ref[...], -SWIGLU_LIMIT, SWIGLU_LIMIT)
        act = (gate * jax.nn.sigmoid(SWIGLU_ALPHA * gate) * (up + 1.0)).astype(bf16)
        part = _dot(act, wd_ref[...].astype(bf16))

        @pl.when(f == 0)
        def _():
            acc_ref[...] = part

        @pl.when(f > 0)
        def _():
            acc_ref[...] += part

    @pl.when(jnp.logical_and(used, f == n_f - 1))
    def _():
        _store_token_major(obuf, (slot,), acc_ref[...] + bd_ref[...])

    @pl.when(i + 1 < n_used)
    def _():
        rows(f * per_step, per_step, lambda j: gather_row(tok_nxt_ref, j, other).start())

    @pl.when(has_prev)
    def _():
        rows(f * per_step, per_step, lambda j: scatter_row(j, other).start())

    @pl.when(jnp.logical_and(has_prev, f == n_f - 1))
    def _():
        rows(0, tb, lambda j: scatter_row(j, other).wait())


def _experts(xt, row_tok, row_dst, block_e, n_used, layer, w_gu, b_gu, w_down, b_down, tb, n_out_rows):
    _, n_exp, d, two_f = w_gu.shape
    groups = d // LANES
    dff = two_f // 2
    tf = _tile(dff, (512, 256, 128))
    n_f = dff // tf
    n_blk = row_tok.shape[0] // tb
    assert tb % n_f == 0

    def f_eff(i, f, nused):
        return jnp.where(i < nused[0], f, n_f - 1)

    smem_blk = lambda imap: pl.BlockSpec((1, 1, tb), imap, memory_space=pltpu.SMEM)
    grid_spec = pltpu.PrefetchScalarGridSpec(
        num_scalar_prefetch=2,
        grid=(n_blk, n_f),
        in_specs=[smem_blk(lambda i, f, be, nu: (i, 0, 0)),
                  smem_blk(lambda i, f, be, nu: (jnp.minimum(i + 1, n_blk - 1), 0, 0)),
                  smem_blk(lambda i, f, be, nu: (jnp.maximum(i - 1, 0), 0, 0)),
                  pl.BlockSpec(memory_space=pl.ANY),
                  pl.BlockSpec((None, None, d, tf), lambda i, f, be, nu: (layer, be[i], 0, f_eff(i, f, nu))),
                  pl.BlockSpec((None, None, d, tf), lambda i, f, be, nu: (layer, be[i], 0, n_f + f_eff(i, f, nu))),
                  pl.BlockSpec((None, None, 1, tf), lambda i, f, be, nu: (layer, be[i], 0, f_eff(i, f, nu))),
                  pl.BlockSpec((None, None, 1, tf), lambda i, f, be, nu: (layer, be[i], 0, n_f + f_eff(i, f, nu))),
                  pl.BlockSpec((None, None, tf, d), lambda i, f, be, nu: (layer, be[i], f_eff(i, f, nu), 0)),
                  pl.BlockSpec((None, None, 1, d), lambda i, f, be, nu: (layer, be[i], 0, 0))],
        out_specs=pl.BlockSpec(memory_space=pl.ANY),
        scratch_shapes=[pltpu.VMEM((2, tb * groups, LANES), f32), pltpu.VMEM((tb, d), bf16),
                        pltpu.VMEM((tb, d), f32), pltpu.VMEM((2, tb * groups, LANES), f32),
                        pltpu.SemaphoreType.DMA((2,)), pltpu.SemaphoreType.DMA((2,))],
    )
    n_layers = w_gu.shape[0]
    b_gu4 = b_gu.reshape(n_layers, n_exp, 1, two_f)
    return pl.pallas_call(
        functools.partial(_expert_body, groups=groups),
        grid_spec=grid_spec,
        out_shape=jax.ShapeDtypeStruct((n_out_rows * groups, LANES), f32),
        compiler_params=_params("arbitrary", "arbitrary"),
        name="moe_experts",
    )(block_e, n_used, row_tok.reshape(n_blk, 1, tb), row_tok.reshape(n_blk, 1, tb), row_dst.reshape(n_blk, 1, tb),
      xt, w_gu, w_gu, b_gu4, b_gu4, w_down, b_down.reshape(n_layers, n_exp, 1, d))


def _combine_body(*refs, alpha, groups):
    y_refs = refs[:TOP_K]
    gate_ref, x_ref, g_ref, b_ref, o_ref, y_ref = refs[TOP_K:]
    tm = x_ref.shape[0]
    gates = [gate_ref[:, k:k + 1] for k in range(TOP_K)]
    for s in range(groups):
        acc = _load_token_major(y_refs[0], (), tm, groups, s) * gates[0]
        for k in range(1, TOP_K):
            acc = acc + _load_token_major(y_refs[k], (), tm, groups, s) * gates[k]
        y_ref[:, s * LANES:(s + 1) * LANES] = acc
    o_ref[...] = _layer_norm(alpha * x_ref[...] + y_ref[...], g_ref[...], b_ref[...])


def _combine_ln(ys, gates, x, g, b, *, alpha):
    r, d = x.shape
    groups = d // LANES
    tm = _tile(r, (256, 128, 64, 32, 16, 8))
    n_t = r // tm
    row = lambda i: (i, 0)
    const2 = lambda i: (0, 0)
    plane = lambda k: (lambda i: (k * n_t + i, 0))
    body = functools.partial(_combine_body, alpha=alpha, groups=groups)
    return pl.pallas_call(
        body,
        grid=(n_t,),
        in_specs=[pl.BlockSpec((tm * groups, LANES), plane(k)) for k in range(TOP_K)]
                 + [pl.BlockSpec((tm, TOP_K), row), pl.BlockSpec((tm, d), row),
                    pl.BlockSpec((1, d), const2), pl.BlockSpec((1, d), const2)],
        out_specs=pl.BlockSpec((tm, d), row),
        out_shape=jax.ShapeDtypeStruct((r, d), f32),
        scratch_shapes=[pltpu.VMEM((tm, d), f32)],
        compiler_params=_params("parallel"),
        name="moe_combine_ln",
    )(*([ys] * TOP_K), gates, x, g, b)


def _moe_ln(x, xt, layer, w_router, b_router, w_gu, b_gu, w_down, b_down, g, b, *, alpha):
    r, d = x.shape
    n_exp = w_router.shape[1]
    n_asg = r * TOP_K
    tb = 512 if n_asg >= 512 * 8 else 128
    idx_t, gate_t, rank_t, cnt = _router(x, w_router.T, b_router.reshape(n_exp, 1))
    counts = cnt[:, 0].astype(i32)
    padded = (counts + tb - 1) // tb * tb
    pad_end = jnp.cumsum(padded)
    pad_start = pad_end - padded
    hot = idx_t[None] == jnp.arange(n_exp, dtype=i32)[:, None, None]
    pos = rank_t + jnp.sum(jnp.where(hot, pad_start[:, None, None], 0), 0)
    n_blk = -(-n_asg // tb) + n_exp + 2
    n_rows = n_blk * tb
    asg = jnp.arange(n_asg, dtype=i32)
    inv = jnp.full((n_rows,), -1, i32).at[pos.reshape(-1)].set(asg, unique_indices=True)
    real = inv >= 0
    row_tok = jnp.where(real, inv % r, 0)
    row_dst = jnp.where(real, inv, n_asg + jnp.arange(n_rows, dtype=i32) % tb)
    blk_start = jnp.arange(n_blk, dtype=i32) * tb
    block_e = jnp.minimum(jnp.sum((pad_end[None, :] <= blk_start[:, None]).astype(i32), 1), n_exp - 1)
    n_used = pad_end[-1:] // tb + 1
    ys = _experts(xt, row_tok, row_dst, block_e, n_used, layer, w_gu, b_gu, w_down, b_down, tb, n_asg + tb)
    return _combine_ln(ys, gate_t.T, x, g, b, alpha=alpha)


def _rope_tables(pos, rope):
    def cos_sin(d):
        inv = ROPE_THETA ** (-jnp.arange(0, d, 2, dtype=f32) / d)
        ang = pos.astype(f32)[:, None] * inv[None, :]
        return jnp.cos(ang), jnp.sin(ang)
    c, s = cos_sin(rope)
    mc = jnp.concatenate([c, c] * (LANES // rope), -1)
    ms = jnp.concatenate([-s, s] * (LANES // rope), -1)
    c, s = cos_sin(DIFF_ROT)
    n = pos.shape[0]
    dc = jnp.concatenate([c, c, jnp.ones((n, LANES - DIFF_ROT), f32)], -1)
    ds = jnp.concatenate([-s, s, jnp.zeros((n, LANES - DIFF_ROT), f32)], -1)
    return mc, ms, dc, ds


def kernel(x_prompt, x_sample, cache_mla_ckv, cache_mla_kpe, cache_diff_k, cache_diff_v, state_gla, meta_tokens, w_in_attn, mla_q_norm, mla_w_uq, mla_kv_norm, mla_w_uk, mla_w_uv, diff_lambda, diff_subln, w_out_attn, w_in_gla, gla_w_gate, gla_b_gate, gla_norm, w_out_gla, ln1_g, ln1_b, ln2_g, ln2_b, moe_w_router, moe_b_router, moe_w_gu, moe_b_gu, moe_w_down, moe_b_down):
    bp, seq, d = x_prompt.shape
    bs, ts, _ = x_sample.shape
    n_meta = meta_tokens.shape[0]
    past_len = cache_mla_ckv.shape[2]
    depth = ln1_g.shape[0]
    assert depth == 2 and ts == n_meta and seq % CHUNK == 0
    alpha = (2 * depth) ** 0.25

    n_heads = mla_w_uq.shape[2]
    q_lora = mla_w_uq.shape[1]
    kv_lora = mla_w_uk.shape[1]
    nope = mla_w_uk.shape[3]
    rope = mla_w_uq.shape[3] - nope
    vdim = mla_w_uv.shape[3]
    diff_heads = cache_diff_k.shape[3]
    diff_qk = cache_diff_k.shape[5]
    diff_cols = diff_heads * 2 * diff_qk
    assert nope == LANES and 2 * rope == LANES and diff_qk == LANES and n_heads % 2 == 0
    assert cache_diff_v.shape[4] == 2 * diff_qk and kv_lora % LANES == 0 and q_lora % LANES == 0

    n_f = bp * seq
    row_meta = n_f
    row_samp = n_f + bp * n_meta
    x = jnp.concatenate([x_prompt.reshape(n_f, d),
                         jnp.broadcast_to(meta_tokens.astype(f32)[None], (bp, n_meta, d)).reshape(bp * n_meta, d),
                         x_sample.reshape(bs * ts, d)], 0)
    pos = jnp.concatenate([jnp.tile(n_meta + jnp.arange(seq, dtype=i32), bp),
                           jnp.tile(jnp.arange(n_meta, dtype=i32), bp),
                           jnp.tile(past_len + jnp.arange(ts, dtype=i32), bs)])
    tq = _tile(seq, (512, 256, 128, 64))

    def moe(x, xt, layer):
        return _moe_ln(x, xt, layer, moe_w_router[layer], moe_b_router[layer], moe_w_gu, moe_b_gu, moe_w_down,
                       moe_b_down, ln2_g[layer][None], ln2_b[layer][None], alpha=alpha)

    w_in = w_in_attn[0]
    s1 = q_lora + kv_lora
    s2 = s1 + rope
    w_in_perm = jnp.concatenate([w_in[:, :s1], w_in[:, s2:], w_in[:, s1:s2],
                                 jnp.zeros((d, LANES - rope), f32)], 1).astype(bf16)
    y = _matmul(x, w_in_perm, f32, "attn_in_proj")
    wuq = mla_w_uq[0]
    wuq_perm = jnp.concatenate([wuq[:, :, :nope].reshape(q_lora, n_heads * nope),
                                wuq[:, :, nope:].reshape(q_lora, n_heads * rope)], 1).astype(bf16)
    wuk_t = jnp.transpose(mla_w_uk[0], (1, 2, 0)).astype(bf16)
    wuv = jnp.transpose(mla_w_uv[0], (1, 0, 2)).astype(bf16)
    tabs = _rope_tables(pos, rope)
    qm, ckv, kpe, km, dqb, dk, dkb, dv, dvb = _attn_prep(
        y, tabs, mla_q_norm[0][None], mla_kv_norm[0][None], wuq_perm, wuk_t,
        n_heads=n_heads, q_lora=q_lora, kv_lora=kv_lora, nope=nope, rope=rope, diff_cols=diff_cols,
        diff_heads=diff_heads)

    mla_scale = (nope + rope) ** -0.5
    diff_scale = diff_qk ** -0.5
    lambda_init = 0.8 - 0.6 * math.exp(-0.3 * 0)
    lam_vecs = diff_lambda[0].astype(f32)
    subln = diff_subln[0][None]
    common_d = dict(scale=diff_scale, lambda_init=lambda_init)
    o_mla = jnp.concatenate([
        _mla_flash(qm, km, wuv, n_batch=bp, seq=seq, n_meta=n_meta, scale=mla_scale, kv_lora=kv_lora, tq=tq),
        _mla_small(qm, km, wuv, None, n_seq=bp, tq=n_meta, row0=row_meta, scale=mla_scale, kv_lora=kv_lora),
        _mla_small(qm, km, wuv, (cache_mla_ckv[0], cache_mla_kpe[0]), n_seq=bs, tq=ts, row0=row_samp,
                   scale=mla_scale, kv_lora=kv_lora)], 0)
    past_d = (cache_diff_k[0].reshape(bs, past_len, diff_cols), cache_diff_v[0].reshape(bs, past_len, diff_cols))
    o_dif = jnp.concatenate([
        _diff_flash(dqb, dkb, dvb, lam_vecs, subln, n_batch=bp, seq=seq, n_meta=n_meta, tq=tq, **common_d),
        _diff_small(dqb, dkb, dvb, lam_vecs, subln, None, n_seq=bp, tq=n_meta, row0=row_meta, **common_d),
        _diff_small(dqb, dkb, dvb, lam_vecs, subln, past_d, n_seq=bs, tq=ts, row0=row_samp, **common_d)], 0)
    w_out = w_out_attn[0].astype(bf16)
    mla_out = n_heads * vdim
    x, xt = _proj_ln([o_mla, o_dif], [w_out[:mla_out], w_out[mla_out:]], x, ln1_g[0][None], ln1_b[0][None],
                     alpha=alpha)
    x = moe(x, xt, 0)

    gla_heads = state_gla.shape[2]
    gla_dk, gla_dv = state_gla.shape[3], state_gla.shape[4]
    qk_cols, v_cols = gla_heads * gla_dk, gla_heads * gla_dv
    rank = gla_w_gate.shape[1]
    w_gla = jnp.concatenate([w_in_gla[0], jnp.zeros((d, LANES - rank), f32)], 1).astype(bf16)
    yg = _matmul(x, w_gla, f32, "gla_in_proj")
    w_gate = jnp.concatenate([gla_w_gate[0], jnp.zeros((LANES - rank, qk_cols), f32)], 0)
    b_gate = gla_b_gate[0][None]
    g_norm = gla_norm[0][None]
    gla_kw = dict(n_heads=gla_heads, qk_cols=qk_cols, v_cols=v_cols)
    s_init = jnp.concatenate([jnp.zeros((bp,) + state_gla.shape[2:], f32), state_gla[0].astype(f32)], 0)
    o_short, s_short = _gla(yg, w_gate, b_gate, g_norm, s_init, n_seq=bp + bs, n_chunks=1, chunk=n_meta,
                            row0=row_meta, **gla_kw)
    o_frames, s_frames = _gla(yg, w_gate, b_gate, g_norm, s_short[:bp], n_seq=bp, n_chunks=seq // GLA_CHUNK,
                              chunk=GLA_CHUNK, row0=0, **gla_kw)
    o_gla = jnp.concatenate([o_frames, o_short], 0)
    x, xt = _proj_ln([o_gla], [w_out_gla[0].astype(bf16)], x, ln1_g[1][None], ln1_b[1][None], alpha=alpha)
    x = moe(x, xt, 1)

    def rows_p(a, tail):
        c = a.shape[1]
        return jnp.concatenate([a[row_meta:row_samp].reshape(bp, n_meta, c), a[:n_f].reshape(bp, seq, c)],
                               1).reshape((1, bp, n_meta + seq) + tail)

    def rows_s(a, tail):
        return a[row_samp:].reshape((1, bs, ts) + tail)

    kpe = kpe[:, :rope]
    t_dk, t_dv = (diff_heads, 2, diff_qk), (diff_heads, 2 * diff_qk)
    return (x[:n_f].reshape(bp, seq, d), x[row_samp:].reshape(bs, ts, d),
            rows_p(ckv, (kv_lora,)), rows_p(kpe, (rope,)), rows_p(dk, t_dk), rows_p(dv, t_dv), s_frames[None],
            rows_s(ckv, (kv_lora,)), rows_s(kpe, (rope,)), rows_s(dk, t_dk), rows_s(dv, t_dv), s_short[bp:][None])
```

```python
import functools
import math

import jax
import jax.numpy as jnp
from jax import lax
from jax.experimental import pallas as pl
from jax.experimental.pallas import tpu as pltpu

f32 = jnp.float32
bf16 = jnp.bfloat16
i32 = jnp.int32

CHUNK = 64
ROPE_THETA = 500000.0
GLA_TAU = 16.0
GLA_CHUNK = 64
TOP_K = 4
SWIGLU_LIMIT = 7.0
SWIGLU_ALPHA = 1.702
DIFF_ROT = 32
NEG_BIG = -1e30
ROW_DMA_UNROLL = 8

LANES = 128
VMEM_LIMIT_BYTES = 56 * 1024 * 1024
EXPERT_TILE_BYTES = 4 * 1024 * 1024


def _params(*sem):
    return pltpu.CompilerParams(dimension_semantics=sem, vmem_limit_bytes=VMEM_LIMIT_BYTES)


def _tile(n, prefs):
    for t in prefs:
        if n % t == 0:
            return t
    raise ValueError(f"no tile in {prefs} divides {n}")


def _dot(a, b, precision=None):
    return jnp.dot(a, b, preferred_element_type=f32, precision=precision)


def _dot_nt(a, b, precision=None):
    return lax.dot_general(a, b, (((1,), (1,)), ((), ())), preferred_element_type=f32, precision=precision)


def _rms(x, g, eps=1e-6):
    return x * lax.rsqrt(jnp.mean(x * x, -1, keepdims=True) + eps) * g


def _layer_norm(x, g, b, eps=1e-5):
    mu = jnp.mean(x, -1, keepdims=True)
    xc = x - mu
    var = jnp.mean(xc * xc, -1, keepdims=True)
    return xc * lax.rsqrt(var + eps) * g + b


def _to_token_major(value):
    tm, d = value.shape
    groups = d // LANES
    parts = jnp.stack([value[:, s * LANES:(s + 1) * LANES] for s in range(groups)], 0)
    return jnp.swapaxes(parts, 0, 1).reshape(tm * groups, LANES)


def _from_token_major(value, tm):
    groups = value.shape[0] // tm
    return jnp.swapaxes(value.reshape(tm, groups, LANES), 0, 1)


def _mm_body(x_ref, w_ref, o_ref):
    o_ref[...] = _dot(x_ref[...].astype(bf16), w_ref[...].astype(bf16)).astype(o_ref.dtype)


def _matmul(x, w, out_dtype, name):
    m, k = x.shape
    n = w.shape[1]
    tm = _tile(m, (512, 256, 128, 64, 32, 16, 8))
    tn = _tile(n, (1408, 1024, 896, 768, 640, 512, 384, 256, 128))
    return pl.pallas_call(
        _mm_body,
        grid=(m // tm, n // tn),
        in_specs=[pl.BlockSpec((tm, k), lambda i, j: (i, 0)),
                  pl.BlockSpec((k, tn), lambda i, j: (0, j))],
        out_specs=pl.BlockSpec((tm, tn), lambda i, j: (i, j)),
        out_shape=jax.ShapeDtypeStruct((m, n), out_dtype),
        compiler_params=_params("parallel", "parallel"),
        name=name,
    )(x, w)


def _attn_prep_body(y_ref, mc_ref, ms_ref, dc_ref, ds_ref, qn_ref, kvn_ref, wuq_ref, wuk_ref,
                    qm_ref, ckv_ref, kpe_ref, km_ref, dqb_ref, dk_ref, dkb_ref, dv_ref, dvb_ref,
                    *, n_heads, q_lora, kv_lora, nope, rope, diff_cols):
    tm = y_ref.shape[0]
    lane = lax.broadcasted_iota(i32, (tm, LANES), 1)
    mc, ms, dc, ds = mc_ref[...], ms_ref[...], dc_ref[...], ds_ref[...]
    half = rope // 2

    def rope_pair(v):
        rolled = jnp.where((lane % rope) < half, pltpu.roll(v, LANES - half, 1), pltpu.roll(v, half, 1))
        return v * mc + rolled * ms

    dhalf = DIFF_ROT // 2

    def rope_diff(v):
        rolled = jnp.where(lane < dhalf, pltpu.roll(v, LANES - dhalf, 1), pltpu.roll(v, dhalf, 1))
        return v * dc + rolled * ds

    qn = _rms(y_ref[:, 0:q_lora], qn_ref[...]).astype(bf16)
    q = _dot(qn, wuq_ref[...])
    for h in range(n_heads):
        lat = _dot(q[:, h * nope:(h + 1) * nope].astype(bf16), wuk_ref[h])
        qm_ref[h, :, 0:kv_lora] = lat.astype(bf16)
    base = n_heads * nope
    for g in range(n_heads // 2):
        pe = rope_pair(q[:, base + g * LANES: base + (g + 1) * LANES]).astype(bf16)
        qm_ref[2 * g, :, kv_lora:kv_lora + rope] = pe[:, 0:rope]
        qm_ref[2 * g + 1, :, kv_lora:kv_lora + rope] = pe[:, rope:2 * rope]
    ckv = _rms(y_ref[:, q_lora:q_lora + kv_lora], kvn_ref[...])
    ckv_ref[...] = ckv
    km_ref[:, 0:kv_lora] = ckv.astype(bf16)
    o_dq = q_lora + kv_lora
    o_dk = o_dq + diff_cols
    o_dv = o_dk + diff_cols
    o_pe = o_dv + diff_cols
    kpe = rope_pair(y_ref[:, o_pe:o_pe + LANES])
    kpe_ref[...] = kpe
    km_ref[:, kv_lora:kv_lora + rope] = kpe[:, 0:rope].astype(bf16)
    for g in range(diff_cols // LANES):
        dq = rope_diff(y_ref[:, o_dq + g * LANES:o_dq + (g + 1) * LANES])
        dqb_ref[g] = dq.astype(bf16)
        dk = rope_diff(y_ref[:, o_dk + g * LANES:o_dk + (g + 1) * LANES])
        dk_ref[:, g * LANES:(g + 1) * LANES] = dk
        dkb_ref[g] = dk.astype(bf16)
    dv = y_ref[:, o_dv:o_dv + diff_cols]
    dv_ref[...] = dv
    vw = dvb_ref.shape[2]
    for h in range(dvb_ref.shape[0]):
        dvb_ref[h] = dv[:, h * vw:(h + 1) * vw].astype(bf16)


def _attn_prep(y, tabs, q_norm, kv_norm, wuq, wukT, *, n_heads, q_lora, kv_lora, nope, rope, diff_cols,
               diff_heads):
    r, ycols = y.shape
    tm = _tile(r, (256, 128, 64, 32, 16))
    dm = kv_lora + rope
    n_qk = diff_cols // LANES
    vw = diff_cols // diff_heads
    row = lambda i: (i, 0)
    mid = lambda i: (0, i, 0)
    const2 = lambda i: (0, 0)
    body = functools.partial(_attn_prep_body, n_heads=n_heads, q_lora=q_lora, kv_lora=kv_lora,
                             nope=nope, rope=rope, diff_cols=diff_cols)
    return pl.pallas_call(
        body,
        grid=(r // tm,),
        in_specs=[pl.BlockSpec((tm, ycols), row)]
                 + [pl.BlockSpec((tm, LANES), row)] * 4
                 + [pl.BlockSpec((1, q_lora), const2), pl.BlockSpec((1, kv_lora), const2),
                    pl.BlockSpec(wuq.shape, const2), pl.BlockSpec(wukT.shape, lambda i: (0, 0, 0))],
        out_specs=[pl.BlockSpec((n_heads, tm, dm), mid),
                   pl.BlockSpec((tm, kv_lora), row), pl.BlockSpec((tm, LANES), row),
                   pl.BlockSpec((tm, dm), row), pl.BlockSpec((n_qk, tm, LANES), mid),
                   pl.BlockSpec((tm, diff_cols), row), pl.BlockSpec((n_qk, tm, LANES), mid),
                   pl.BlockSpec((tm, diff_cols), row), pl.BlockSpec((diff_heads, tm, vw), mid)],
        out_shape=[jax.ShapeDtypeStruct((n_heads, r, dm), bf16),
                   jax.ShapeDtypeStruct((r, kv_lora), f32), jax.ShapeDtypeStruct((r, LANES), f32),
                   jax.ShapeDtypeStruct((r, dm), bf16), jax.ShapeDtypeStruct((n_qk, r, LANES), bf16),
                   jax.ShapeDtypeStruct((r, diff_cols), f32), jax.ShapeDtypeStruct((n_qk, r, LANES), bf16),
                   jax.ShapeDtypeStruct((r, diff_cols), f32), jax.ShapeDtypeStruct((diff_heads, r, vw), bf16)],
        compiler_params=_params("parallel"),
        name="attn_prep",
    )(y, *tabs, q_norm, kv_norm, wuq, wukT)


def _chunk_mask(tq, tk):
    rows = lax.broadcasted_iota(i32, (tq, tk), 0)
    cols = lax.broadcasted_iota(i32, (tq, tk), 1)
    return (cols // CHUNK) <= (rows // CHUNK)


def _softmax_step(s, v, m_ref, l_ref, acc_ref, c):
    m_prev = m_ref[c]
    m_new = jnp.maximum(m_prev, jnp.max(s, -1, keepdims=True))
    a = jnp.exp(m_prev - m_new)
    p = jnp.exp(s - m_new)
    l_ref[c] = a * l_ref[c] + jnp.sum(p, -1, keepdims=True)
    acc_ref[c] = a * acc_ref[c] + _dot(p.astype(bf16), v)
    m_ref[c] = m_new


def _init_softmax(m_ref, l_ref, acc_ref):
    m_ref[...] = jnp.full(m_ref.shape, NEG_BIG, f32)
    l_ref[...] = jnp.zeros(l_ref.shape, f32)
    acc_ref[...] = jnp.zeros(acc_ref.shape, f32)


def _mla_flash_body(q_ref, kf_ref, kmeta_ref, wuv_ref, o_ref, m_ref, l_ref, acc_ref, *, scale, kv_lora):
    n_heads, tq, _ = q_ref.shape
    tk = kf_ref.shape[0]
    vdim = wuv_ref.shape[2]
    qi, kj = pl.program_id(1), pl.program_id(2)

    def attend(k_ref, masked):
        mask = _chunk_mask(tq, tk) if masked else None

        def head(h, carry):
            k = k_ref[...]
            s = _dot_nt(q_ref[h], k) * scale
            if masked:
                s = jnp.where(mask, s, NEG_BIG)
            _softmax_step(s, k[:, 0:kv_lora], m_ref, l_ref, acc_ref, h)
            return carry

        lax.fori_loop(0, n_heads, head, 0, unroll=2)

    @pl.when(kj == 0)
    def _():
        _init_softmax(m_ref, l_ref, acc_ref)
        attend(kmeta_ref, False)

    @pl.when(kj < qi)
    def _():
        attend(kf_ref, False)

    @pl.when(kj == qi)
    def _():
        attend(kf_ref, True)

    @pl.when(kj == pl.num_programs(2) - 1)
    def _():
        for h in range(n_heads):
            o = (acc_ref[h] / l_ref[h]).astype(bf16)
            o_ref[:, h * vdim:(h + 1) * vdim] = _dot(o, wuv_ref[h]).astype(o_ref.dtype)


def _mla_flash(qm, km, wuv, *, n_batch, seq, n_meta, scale, kv_lora, tq):
    n_heads, r, dm = qm.shape
    vdim = wuv.shape[2]
    nq = seq // tq
    meta_blk0 = n_batch * seq // n_meta
    body = functools.partial(_mla_flash_body, scale=scale, kv_lora=kv_lora)
    return pl.pallas_call(
        body,
        grid=(n_batch, nq, nq),
        in_specs=[pl.BlockSpec((n_heads, tq, dm), lambda b, i, j: (0, b * nq + i, 0)),
                  pl.BlockSpec((tq, dm), lambda b, i, j: (b * nq + jnp.minimum(i, j), 0)),
                  pl.BlockSpec((n_meta, dm), lambda b, i, j: (meta_blk0 + b, 0)),
                  pl.BlockSpec(wuv.shape, lambda b, i, j: (0, 0, 0))],
        out_specs=pl.BlockSpec((tq, n_heads * vdim), lambda b, i, j: (b * nq + i, 0)),
        out_shape=jax.ShapeDtypeStruct((n_batch * seq, n_heads * vdim), bf16),
        scratch_shapes=[pltpu.VMEM((n_heads, tq, 1), f32), pltpu.VMEM((n_heads, tq, 1), f32),
                        pltpu.VMEM((n_heads, tq, kv_lora), f32)],
        compiler_params=_params("parallel", "parallel", "arbitrary"),
        name="mla_flash",
    )(qm, km, km, wuv)


def _diff_lambda(lam_ref, lambda_init):
    lv = lam_ref[...]
    return (jnp.exp(jnp.sum(lv[0:1] * lv[1:2], keepdims=True))
            - jnp.exp(jnp.sum(lv[2:3] * lv[3:4], keepdims=True)) + lambda_init)


def _diff_flash_body(q_ref, kf_ref, vf_ref, kmeta_ref, vmeta_ref, lam_ref, sub_ref, o_ref,
                     m_ref, l_ref, acc_ref, *, scale, lambda_init):
    n_qk, tq, _ = q_ref.shape
    tk = kf_ref.shape[1]
    vw = vf_ref.shape[2]
    qi, kj = pl.program_id(1), pl.program_id(2)

    def attend(k_ref, v_ref, masked):
        mask = _chunk_mask(tq, tk) if masked else None

        def part(c, carry):
            s = _dot_nt(q_ref[c], k_ref[c]) * scale
            if masked:
                s = jnp.where(mask, s, NEG_BIG)
            _softmax_step(s, v_ref[c // 2], m_ref, l_ref, acc_ref, c)
            return carry

        lax.fori_loop(0, n_qk, part, 0, unroll=2)

    @pl.when(kj == 0)
    def _():
        _init_softmax(m_ref, l_ref, acc_ref)
        attend(kmeta_ref, vmeta_ref, False)

    @pl.when(kj < qi)
    def _():
        attend(kf_ref, vf_ref, False)

    @pl.when(kj == qi)
    def _():
        attend(kf_ref, vf_ref, True)

    @pl.when(kj == pl.num_programs(2) - 1)
    def _():
        lam = _diff_lambda(lam_ref, lambda_init)
        for h in range(n_qk // 2):
            o = acc_ref[2 * h] / l_ref[2 * h] - lam * (acc_ref[2 * h + 1] / l_ref[2 * h + 1])
            o_ref[:, h * vw:(h + 1) * vw] = (_rms(o, sub_ref[...]) * (1.0 - lambda_init)).astype(o_ref.dtype)


def _diff_flash(dqb, dkb, dvb, lam_vecs, subln, *, n_batch, seq, n_meta, scale, lambda_init, tq):
    n_qk, r, d = dqb.shape
    n_heads, _, vw = dvb.shape
    nq = seq // tq
    meta_blk0 = n_batch * seq // n_meta
    body = functools.partial(_diff_flash_body, scale=scale, lambda_init=lambda_init)
    qmap = lambda b, i, j: (0, b * nq + i, 0)
    kmap = lambda b, i, j: (0, b * nq + jnp.minimum(i, j), 0)
    mmap = lambda b, i, j: (0, meta_blk0 + b, 0)
    const2 = lambda b, i, j: (0, 0)
    return pl.pallas_call(
        body,
        grid=(n_batch, nq, nq),
        in_specs=[pl.BlockSpec((n_qk, tq, d), qmap), pl.BlockSpec((n_qk, tq, d), kmap),
                  pl.BlockSpec((n_heads, tq, vw), kmap),
                  pl.BlockSpec((n_qk, n_meta, d), mmap), pl.BlockSpec((n_heads, n_meta, vw), mmap),
                  pl.BlockSpec(lam_vecs.shape, const2), pl.BlockSpec(subln.shape, const2)],
        out_specs=pl.BlockSpec((tq, n_heads * vw), lambda b, i, j: (b * nq + i, 0)),
        out_shape=jax.ShapeDtypeStruct((n_batch * seq, n_heads * vw), bf16),
        scratch_shapes=[pltpu.VMEM((n_qk, tq, 1), f32), pltpu.VMEM((n_qk, tq, 1), f32),
                        pltpu.VMEM((n_qk, tq, vw), f32)],
        compiler_params=_params("parallel", "parallel", "arbitrary"),
        name="diff_flash",
    )(dqb, dkb, dvb, dkb, dvb, lam_vecs, subln)


def _mla_small_body(*refs, scale, kv_lora, has_past):
    if has_past:
        q_ref, kown_ref, ckvp_ref, kpep_ref, wuv_ref, o_ref = refs
    else:
        q_ref, kown_ref, wuv_ref, o_ref = refs
    n_heads, tq, dm = q_ref.shape
    vdim = wuv_ref.shape[2]
    q = q_ref[...].reshape(n_heads * tq, dm)
    kown = kown_ref[...]
    s_own = _dot_nt(q, kown) * scale
    m = jnp.max(s_own, -1, keepdims=True)
    if has_past:
        kp = ckvp_ref[0].astype(bf16)
        s_past = (_dot_nt(q[:, 0:kv_lora], kp) + _dot_nt(q[:, kv_lora:dm], kpep_ref[0].astype(bf16))) * scale
        m = jnp.maximum(m, jnp.max(s_past, -1, keepdims=True))
    p_own = jnp.exp(s_own - m)
    l = jnp.sum(p_own, -1, keepdims=True)
    acc = _dot(p_own.astype(bf16), kown[:, 0:kv_lora])
    if has_past:
        p_past = jnp.exp(s_past - m)
        l = l + jnp.sum(p_past, -1, keepdims=True)
        acc = acc + _dot(p_past.astype(bf16), kp)
    o = (acc / l).astype(bf16)
    for h in range(n_heads):
        o_ref[:, h * vdim:(h + 1) * vdim] = _dot(o[h * tq:(h + 1) * tq], wuv_ref[h]).astype(o_ref.dtype)


def _mla_small(qm, km, wuv, past, *, n_seq, tq, row0, scale, kv_lora):
    n_heads, r, dm = qm.shape
    vdim = wuv.shape[2]
    blk0 = row0 // tq
    has_past = past is not None
    in_specs = [pl.BlockSpec((n_heads, tq, dm), lambda b: (0, blk0 + b, 0)),
                pl.BlockSpec((tq, dm), lambda b: (blk0 + b, 0))]
    args = [qm, km]
    if has_past:
        ckvp, kpep = past
        in_specs += [pl.BlockSpec((1,) + ckvp.shape[1:], lambda b: (b, 0, 0)),
                     pl.BlockSpec((1,) + kpep.shape[1:], lambda b: (b, 0, 0))]
        args += [ckvp, kpep]
    in_specs.append(pl.BlockSpec(wuv.shape, lambda b: (0, 0, 0)))
    args.append(wuv)
    body = functools.partial(_mla_small_body, scale=scale, kv_lora=kv_lora, has_past=has_past)
    return pl.pallas_call(
        body,
        grid=(n_seq,),
        in_specs=in_specs,
        out_specs=pl.BlockSpec((tq, n_heads * vdim), lambda b: (b, 0)),
        out_shape=jax.ShapeDtypeStruct((n_seq * tq, n_heads * vdim), bf16),
        compiler_params=_params("parallel"),
        name="mla_small_past" if has_past else "mla_small",
    )(*args)


def _diff_small_body(*refs, scale, lambda_init, has_past):
    if has_past:
        q_ref, kown_ref, vown_ref, kp_ref, vp_ref, lam_ref, sub_ref, o_ref = refs
    else:
        q_ref, kown_ref, vown_ref, lam_ref, sub_ref, o_ref = refs
    d = q_ref.shape[2]
    outs = []
    for i in range(2):
        q = q_ref[i]
        s_own = _dot_nt(q, kown_ref[i]) * scale
        m = jnp.max(s_own, -1, keepdims=True)
        if has_past:
            s_past = _dot_nt(q, kp_ref[0, :, i * d:(i + 1) * d].astype(bf16)) * scale
            m = jnp.maximum(m, jnp.max(s_past, -1, keepdims=True))
        p_own = jnp.exp(s_own - m)
        l = jnp.sum(p_own, -1, keepdims=True)
        acc = _dot(p_own.astype(bf16), vown_ref[0])
        if has_past:
            p_past = jnp.exp(s_past - m)
            l = l + jnp.sum(p_past, -1, keepdims=True)
            acc = acc + _dot(p_past.astype(bf16), vp_ref[0].astype(bf16))
        outs.append(acc / l)
    o = outs[0] - _diff_lambda(lam_ref, lambda_init) * outs[1]
    o_ref[...] = (_rms(o, sub_ref[...]) * (1.0 - lambda_init)).astype(o_ref.dtype)


def _diff_small(dqb, dkb, dvb, lam_vecs, subln, past, *, n_seq, tq, row0, scale, lambda_init):
    n_qk, r, d = dqb.shape
    n_heads, _, vw = dvb.shape
    blk0 = row0 // tq
    has_past = past is not None
    own = lambda b, h: (h, blk0 + b, 0)
    in_specs = [pl.BlockSpec((2, tq, d), own), pl.BlockSpec((2, tq, d), own), pl.BlockSpec((1, tq, vw), own)]
    args = [dqb, dkb, dvb]
    if has_past:
        kp, vp = past
        in_specs += [pl.BlockSpec((1, kp.shape[1], vw), lambda b, h: (b, 0, h)),
                     pl.BlockSpec((1, vp.shape[1], vw), lambda b, h: (b, 0, h))]
        args += [kp, vp]
    in_specs += [pl.BlockSpec(lam_vecs.shape, lambda b, h: (0, 0)), pl.BlockSpec(subln.shape, lambda b, h: (0, 0))]
    args += [lam_vecs, subln]
    body = functools.partial(_diff_small_body, scale=scale, lambda_init=lambda_init, has_past=has_past)
    return pl.pallas_call(
        body,
        grid=(n_seq, n_heads),
        in_specs=in_specs,
        out_specs=pl.BlockSpec((tq, vw), lambda b, h: (b, h)),
        out_shape=jax.ShapeDtypeStruct((n_seq * tq, n_heads * vw), bf16),
        compiler_params=_params("parallel", "parallel"),
        name="diff_small_past" if has_past else "diff_small",
    )(*args)


def _gla_body(q_ref, k_ref, v_ref, r_ref, g_ref, wg_ref, bg_ref, gn_ref, s0_ref, o_ref, sout_ref, s_ref,
              *, n_heads):
    c = q_ref.shape[0]
    dk = q_ref.shape[1] // n_heads
    dv = v_ref.shape[1] // n_heads
    step = pl.program_id(1)

    @pl.when(step == 0)
    def _():
        s_ref[...] = s0_ref[0]

    pre = _dot(g_ref[...], wg_ref[...], precision=lax.Precision.HIGHEST) + bg_ref[...]
    log_a = jax.nn.log_sigmoid(pre) / GLA_TAU
    rows = lax.broadcasted_iota(i32, (c, c), 0)
    cols = lax.broadcasted_iota(i32, (c, c), 1)
    causal = rows >= cols
    tril = causal.astype(f32)
    for h in range(n_heads):
        la = log_a[:, h * dk:(h + 1) * dk]
        b = _dot(tril, la, precision=lax.Precision.HIGHEST)
        b_t = b.T
        b_last = b_t[:, c - 1:c]
        qh = q_ref[:, h * dk:(h + 1) * dk] * (dk ** -0.5)
        kh = k_ref[:, h * dk:(h + 1) * dk]
        vh = v_ref[:, h * dv:(h + 1) * dv].astype(bf16)
        q_in = (qh * jnp.exp(b)).astype(bf16)
        k_out = (kh * jnp.exp(-b)).astype(bf16)
        scores = jnp.where(causal, _dot_nt(q_in, k_out), 0.0)
        s_prev = s_ref[h]
        o = _dot(scores.astype(bf16), vh) + _dot(q_in, s_prev.astype(bf16))
        k_dec_t = (kh.T * jnp.exp(b_last - b_t)).astype(bf16)
        s_ref[h] = jnp.exp(b_last) * s_prev + _dot(k_dec_t, vh)
        gate = jax.nn.silu(r_ref[:, h * dv:(h + 1) * dv])
        o_ref[:, h * dv:(h + 1) * dv] = (_rms(o, gn_ref[...]) * gate).astype(o_ref.dtype)

    @pl.when(step == pl.num_programs(1) - 1)
    def _():
        sout_ref[0] = s_ref[...]


def _gla(y, w_gate, b_gate, g_norm, s0, *, n_seq, n_chunks, chunk, row0, n_heads, qk_cols, v_cols):
    blk0 = row0 // chunk
    rmap = lambda col: (lambda b, s: (blk0 + b * n_chunks + s, col))
    g_col = (2 * qk_cols + 2 * v_cols) // LANES
    const2 = lambda b, s: (0, 0)
    smap = lambda b, s: (b, 0, 0, 0)
    body = functools.partial(_gla_body, n_heads=n_heads)
    return pl.pallas_call(
        body,
        grid=(n_seq, n_chunks),
        in_specs=[pl.BlockSpec((chunk, qk_cols), rmap(0)), pl.BlockSpec((chunk, qk_cols), rmap(1)),
                  pl.BlockSpec((chunk, v_cols), rmap(2 * qk_cols // v_cols)),
                  pl.BlockSpec((chunk, v_cols), rmap(2 * qk_cols // v_cols + 1)),
                  pl.BlockSpec((chunk, LANES), rmap(g_col)),
                  pl.BlockSpec(w_gate.shape, const2), pl.BlockSpec(b_gate.shape, const2),
                  pl.BlockSpec(g_norm.shape, const2),
                  pl.BlockSpec((1,) + s0.shape[1:], smap)],
        out_specs=[pl.BlockSpec((chunk, v_cols), lambda b, s: (b * n_chunks + s, 0)),
                   pl.BlockSpec((1,) + s0.shape[1:], smap)],
        out_shape=[jax.ShapeDtypeStruct((n_seq * n_chunks * chunk, v_cols), bf16),
                   jax.ShapeDtypeStruct(s0.shape, f32)],
        scratch_shapes=[pltpu.VMEM(s0.shape[1:], f32)],
        compiler_params=_params("parallel", "arbitrary"),
        name=f"gla_chunk{chunk}",
    )(y, y, y, y, y, w_gate, b_gate, g_norm, s0)


def _proj_ln_body(*refs, n_in, alpha):
    a_refs = refs[:n_in]
    w_refs = refs[n_in:2 * n_in]
    x_ref, g_ref, b_ref, o_ref, ot_ref = refs[2 * n_in:]
    h = _dot(a_refs[0][...], w_refs[0][...])
    for a_ref, w_ref in zip(a_refs[1:], w_refs[1:]):
        h = h + _dot(a_ref[...], w_ref[...])
    out = _layer_norm(alpha * x_ref[...] + h, g_ref[...], b_ref[...])
    o_ref[...] = out
    ot_ref[...] = _to_token_major(out)


def _proj_ln(acts, weights, x, g, b, *, alpha):
    r, d = x.shape
    groups = d // LANES
    tm = _tile(r, (256, 128, 64, 32, 16, 8))
    row = lambda i: (i, 0)
    const2 = lambda i: (0, 0)
    body = functools.partial(_proj_ln_body, n_in=len(acts), alpha=alpha)
    return pl.pallas_call(
        body,
        grid=(r // tm,),
        in_specs=[pl.BlockSpec((tm, a.shape[1]), row) for a in acts]
                 + [pl.BlockSpec(w.shape, const2) for w in weights]
                 + [pl.BlockSpec((tm, d), row), pl.BlockSpec((1, d), const2), pl.BlockSpec((1, d), const2)],
        out_specs=[pl.BlockSpec((tm, d), row), pl.BlockSpec((tm * groups, LANES), row)],
        out_shape=[jax.ShapeDtypeStruct((r, d), f32), jax.ShapeDtypeStruct((r * groups, LANES), f32)],
        compiler_params=_params("parallel"),
        name="proj_ln",
    )(*acts, *weights, x, g, b)


def _router_body(x_ref, w_ref, b_ref, idx_ref, gate_ref, rank_ref, cnt_ref, base_ref):
    n_exp = w_ref.shape[0]
    tm = x_ref.shape[0]

    @pl.when(pl.program_id(0) == 0)
    def _():
        base_ref[...] = jnp.zeros(base_ref.shape, f32)

    logits = _dot_nt(w_ref[...], x_ref[...], precision=lax.Precision.HIGHEST) + b_ref[...]
    eid = lax.broadcasted_iota(i32, (n_exp, tm), 0).astype(f32)
    work = logits
    vals, hots = [], []
    for k in range(TOP_K):
        m = jnp.max(work, 0, keepdims=True)
        sel = jnp.min(jnp.where(work == m, eid, float(n_exp)), 0, keepdims=True)
        hot = eid == sel
        idx_ref[k:k + 1, :] = sel.astype(i32)
        vals.append(m)
        hots.append(hot)
        work = jnp.where(hot, -jnp.inf, work)
    exps = [jnp.exp(v - vals[0]) for v in vals]
    denom = functools.reduce(jnp.add, exps)
    for k in range(TOP_K):
        gate_ref[k:k + 1, :] = exps[k] / denom
    hot_all = functools.reduce(jnp.logical_or, hots).astype(bf16)
    t_row = lax.broadcasted_iota(i32, (tm, tm), 0)
    t_col = lax.broadcasted_iota(i32, (tm, tm), 1)
    before = (t_row < t_col).astype(bf16)
    prefix = _dot(hot_all, before) + base_ref[...]
    for k in range(TOP_K):
        rank_ref[k:k + 1, :] = jnp.sum(jnp.where(hots[k], prefix, 0.0), 0, keepdims=True).astype(i32)
    base_ref[...] = base_ref[...] + jnp.sum(hot_all.astype(f32), 1, keepdims=True)
    cnt_ref[...] = jnp.broadcast_to(base_ref[...], cnt_ref.shape)


def _router(x, w_t, b_col):
    r, d = x.shape
    n_exp = w_t.shape[0]
    tm = _tile(r, (512, 256, 128))
    col = lambda i: (0, i)
    const2 = lambda i: (0, 0)
    return pl.pallas_call(
        _router_body,
        grid=(r // tm,),
        in_specs=[pl.BlockSpec((tm, d), lambda i: (i, 0)), pl.BlockSpec((n_exp, d), const2),
                  pl.BlockSpec((n_exp, 1), const2)],
        out_specs=[pl.BlockSpec((TOP_K, tm), col), pl.BlockSpec((TOP_K, tm), col), pl.BlockSpec((TOP_K, tm), col),
                   pl.BlockSpec((n_exp, LANES), const2)],
        out_shape=[jax.ShapeDtypeStruct((TOP_K, r), i32), jax.ShapeDtypeStruct((TOP_K, r), f32),
                   jax.ShapeDtypeStruct((TOP_K, r), i32), jax.ShapeDtypeStruct((n_exp, LANES), f32)],
        scratch_shapes=[pltpu.VMEM((n_exp, 1), f32)],
        compiler_params=_params("arbitrary"),
        name="router",
    )(x, w_t, b_col)


def _expert_body(be_ref, nused_ref, tok_cur_ref, tok_nxt_ref, dst_prev_ref,
                 x_hbm, wg_ref, wu_ref, bg_ref, bu_ref, wd_ref, bd_ref, ys_hbm,
                 gbuf, xb_ref, acc_ref, obuf, gsem, ssem, *, groups):
    i, f = pl.program_id(0), pl.program_id(1)
    n_f = pl.num_programs(1)
    n_used = nused_ref[0]
    tb = xb_ref.shape[0]
    per_step = tb // n_f
    slot = lax.rem(i, 2)
    other = 1 - slot
    used = i < n_used
    has_prev = jnp.logical_and(i >= 1, i <= n_used)

    def gather_row(tok_ref, j, s):
        src = pl.multiple_of(tok_ref[0, 0, j] * groups, groups)
        dst = pl.multiple_of(j * groups, groups)
        return pltpu.make_async_copy(x_hbm.at[pl.ds(src, groups), :], gbuf.at[s, pl.ds(dst, groups), :],
                                     gsem.at[s])

    def scatter_row(j, s):
        src = pl.multiple_of(j * groups, groups)
        dst = pl.multiple_of(dst_prev_ref[0, 0, j] * groups, groups)
        return pltpu.make_async_copy(obuf.at[s, pl.ds(src, groups), :], ys_hbm.at[pl.ds(dst, groups), :],
                                     ssem.at[s])

    def rows(first, count, fn):
        def step(j, carry):
            fn(first + j)
            return carry
        lax.fori_loop(0, count, step, 0, unroll=ROW_DMA_UNROLL)

    @pl.when(jnp.logical_and(i == 0, f == 0))
    def _():
        rows(0, tb, lambda j: gather_row(tok_cur_ref, j, 0).start())

    @pl.when(i + 1 < n_used)
    def _():
        rows(f * per_step, per_step, lambda j: gather_row(tok_nxt_ref, j, other).start())

    @pl.when(has_prev)
    def _():
        rows(f * per_step, per_step, lambda j: scatter_row(j, other).start())

    @pl.when(jnp.logical_and(used, f == 0))
    def _():
        rows(0, tb, lambda j: gather_row(tok_cur_ref, j, slot).wait())
        g3 = _from_token_major(gbuf[slot], tb)
        for s in range(groups):
            xb_ref[:, s * LANES:(s + 1) * LANES] = g3[s].astype(bf16)

    @pl.when(used)
    def _():
        x = xb_ref[...]
        gate = jnp.minimum(_dot(x, wg_ref[...].astype(bf16)) + bg_ref[...], SWIGLU_LIMIT)
        up = jnp.clip(_dot(x, wu_ref[...].astype(bf16)) + bu_ref[...], -SWIGLU_LIMIT, SWIGLU_LIMIT)
        act = (gate * jax.nn.sigmoid(SWIGLU_ALPHA * gate) * (up + 1.0)).astype(bf16)
        part = _dot(act, wd_ref[...].astype(bf16))

        @pl.when(f == 0)
        def _():
            acc_ref[...] = part

        @pl.when(f > 0)
        def _():
            acc_ref[...] += part

    @pl.when(jnp.logical_and(used, f == n_f - 1))
    def _():
        obuf[slot] = _to_token_major(acc_ref[...] + bd_ref[...])

    @pl.when(jnp.logical_and(has_prev, f == n_f - 1))
    def _():
        rows(0, tb, lambda j: scatter_row(j, other).wait())


def _experts(xt, row_tok, row_dst, block_e, n_used, layer, w_gu, b_gu, w_down, b_down, tb, n_out_rows):
    _, n_exp, d, two_f = w_gu.shape
    groups = d // LANES
    dff = two_f // 2
    tf = _tile(dff, tuple(t for t in (1024, 512, 256, 128) if d * t * w_gu.dtype.itemsize <= EXPERT_TILE_BYTES))
    n_f = dff // tf
    n_blk = row_tok.shape[0] // tb
    assert tb % n_f == 0

    def f_eff(i, f, nused):
        return jnp.where(i < nused[0], f, n_f - 1)

    smem_blk = lambda imap: pl.BlockSpec((1, 1, tb), imap, memory_space=pltpu.SMEM)
    grid_spec = pltpu.PrefetchScalarGridSpec(
        num_scalar_prefetch=2,
        grid=(n_blk, n_f),
        in_specs=[smem_blk(lambda i, f, be, nu: (i, 0, 0)),
                  smem_blk(lambda i, f, be, nu: (jnp.minimum(i + 1, n_blk - 1), 0, 0)),
                  smem_blk(lambda i, f, be, nu: (jnp.maximum(i - 1, 0), 0, 0)),
                  pl.BlockSpec(memory_space=pl.ANY),
                  pl.BlockSpec((None, None, d, tf), lambda i, f, be, nu: (layer, be[i], 0, f_eff(i, f, nu))),
                  pl.BlockSpec((None, None, d, tf), lambda i, f, be, nu: (layer, be[i], 0, n_f + f_eff(i, f, nu))),
                  pl.BlockSpec((None, None, 1, tf), lambda i, f, be, nu: (layer, be[i], 0, f_eff(i, f, nu))),
                  pl.BlockSpec((None, None, 1, tf), lambda i, f, be, nu: (layer, be[i], 0, n_f + f_eff(i, f, nu))),
                  pl.BlockSpec((None, None, tf, d), lambda i, f, be, nu: (layer, be[i], f_eff(i, f, nu), 0)),
                  pl.BlockSpec((None, None, 1, d), lambda i, f, be, nu: (layer, be[i], 0, 0))],
        out_specs=pl.BlockSpec(memory_space=pl.ANY),
        scratch_shapes=[pltpu.VMEM((2, tb * groups, LANES), f32), pltpu.VMEM((tb, d), bf16),
                        pltpu.VMEM((tb, d), f32), pltpu.VMEM((2, tb * groups, LANES), f32),
                        pltpu.SemaphoreType.DMA((2,)), pltpu.SemaphoreType.DMA((2,))],
    )
    n_layers = w_gu.shape[0]
    b_gu4 = b_gu.reshape(n_layers, n_exp, 1, two_f)
    return pl.pallas_call(
        functools.partial(_expert_body, groups=groups),
        grid_spec=grid_spec,
        out_shape=jax.ShapeDtypeStruct((n_out_rows * groups, LANES), f32),
        compiler_params=_params("arbitrary", "arbitrary"),
        name="moe_experts",
    )(block_e, n_used, row_tok.reshape(n_blk, 1, tb), row_tok.reshape(n_blk, 1, tb), row_dst.reshape(n_blk, 1, tb),
      xt, w_gu, w_gu, b_gu4, b_gu4, w_down, b_down.reshape(n_layers, n_exp, 1, d))


def _combine_body(*refs, alpha, groups, head_tiles, tail_tile0):
    y_refs = refs[:TOP_K]
    gate_ref, x_ref, g_ref, b_ref = refs[TOP_K:TOP_K + 4]
    out_refs, y_ref = refs[TOP_K + 4:-1], refs[-1]
    tm = x_ref.shape[0]
    gates = [gate_ref[:, k:k + 1] for k in range(TOP_K)]
    planes = [_from_token_major(y_refs[k][...], tm) for k in range(TOP_K)]
    for s in range(groups):
        acc = planes[0][s] * gates[0]
        for k in range(1, TOP_K):
            acc = acc + planes[k][s] * gates[k]
        y_ref[:, s * LANES:(s + 1) * LANES] = acc
    out = _layer_norm(alpha * x_ref[...] + y_ref[...], g_ref[...], b_ref[...])
    if head_tiles is None:
        out_refs[0][...] = out
    else:
        i = pl.program_id(0)

        @pl.when(i < head_tiles)
        def _():
            out_refs[0][...] = out

        @pl.when(i >= tail_tile0)
        def _():
            out_refs[1][...] = out


def _combine_ln(ys, gates, x, g, b, *, alpha, head_rows=None, tail_row0=None):
    r, d = x.shape
    groups = d // LANES
    tm = _tile(r, (256, 128, 64, 32, 16, 8))
    split = head_rows is not None and head_rows % tm == 0 and tail_row0 % tm == 0
    n_t = r // tm
    row = lambda i: (i, 0)
    const2 = lambda i: (0, 0)
    plane = lambda k: (lambda i: (k * n_t + i, 0))
    if split:
        head_tiles, tail_tile0 = head_rows // tm, tail_row0 // tm
        out_specs = [pl.BlockSpec((tm, d), lambda i: (jnp.minimum(i, head_tiles - 1), 0)),
                     pl.BlockSpec((tm, d), lambda i: (jnp.maximum(i - tail_tile0, 0), 0))]
        out_shape = [jax.ShapeDtypeStruct((head_rows, d), f32), jax.ShapeDtypeStruct((r - tail_row0, d), f32)]
    else:
        head_tiles = tail_tile0 = None
        out_specs = [pl.BlockSpec((tm, d), row)]
        out_shape = [jax.ShapeDtypeStruct((r, d), f32)]
    body = functools.partial(_combine_body, alpha=alpha, groups=groups, head_tiles=head_tiles,
                             tail_tile0=tail_tile0)
    outs = pl.pallas_call(
        body,
        grid=(n_t,),
        in_specs=[pl.BlockSpec((tm * groups, LANES), plane(k)) for k in range(TOP_K)]
                 + [pl.BlockSpec((tm, TOP_K), row), pl.BlockSpec((tm, d), row),
                    pl.BlockSpec((1, d), const2), pl.BlockSpec((1, d), const2)],
        out_specs=out_specs,
        out_shape=out_shape,
        scratch_shapes=[pltpu.VMEM((tm, d), f32)],
        compiler_params=_params("arbitrary"),
        name="moe_combine_ln",
    )(*([ys] * TOP_K), gates, x, g, b)
    if split:
        return outs
    if head_rows is None:
        return outs[0]
    return outs[0][:head_rows], outs[0][tail_row0:]


def _moe_ln(x, xt, layer, w_router, b_router, w_gu, b_gu, w_down, b_down, g, b, *, alpha, **split):
    r, d = x.shape
    n_exp = w_router.shape[1]
    n_asg = r * TOP_K
    tb = 512 if n_asg >= 512 * 8 else 128
    idx_t, gate_t, rank_t, cnt = _router(x, w_router.T, b_router.reshape(n_exp, 1))
    counts = cnt[:, 0].astype(i32)
    padded = (counts + tb - 1) // tb * tb
    pad_end = jnp.cumsum(padded)
    pad_start = pad_end - padded
    hot = idx_t[None] == jnp.arange(n_exp, dtype=i32)[:, None, None]
    pos = rank_t + jnp.sum(jnp.where(hot, pad_start[:, None, None], 0), 0)
    n_blk = -(-n_asg // tb) + n_exp + 2
    n_rows = n_blk * tb
    asg = jnp.arange(n_asg, dtype=i32)
    inv = jnp.full((n_rows,), -1, i32).at[pos.reshape(-1)].set(asg, unique_indices=True)
    real = inv >= 0
    row_tok = jnp.where(real, inv % r, 0)
    row_dst = jnp.where(real, inv, n_asg + jnp.arange(n_rows, dtype=i32) % tb)
    blk_start = jnp.arange(n_blk, dtype=i32) * tb
    block_e = jnp.minimum(jnp.sum((pad_end[None, :] <= blk_start[:, None]).astype(i32), 1), n_exp - 1)
    n_used = pad_end[-1:] // tb + 1
    ys = _experts(xt, row_tok, row_dst, block_e, n_used, layer, w_gu, b_gu, w_down, b_down, tb, n_asg + tb)
    return _combine_ln(ys, gate_t.T, x, g, b, alpha=alpha, **split)


def _rope_tables(pos, rope):
    def cos_sin(d):
        inv = ROPE_THETA ** (-jnp.arange(0, d, 2, dtype=f32) / d)
        ang = pos.astype(f32)[:, None] * inv[None, :]
        return jnp.cos(ang), jnp.sin(ang)
    c, s = cos_sin(rope)
    mc = jnp.concatenate([c, c] * (LANES // rope), -1)
    ms = jnp.concatenate([-s, s] * (LANES // rope), -1)
    c, s = cos_sin(DIFF_ROT)
    n = pos.shape[0]
    dc = jnp.concatenate([c, c, jnp.ones((n, LANES - DIFF_ROT), f32)], -1)
    ds = jnp.concatenate([-s, s, jnp.zeros((n, LANES - DIFF_ROT), f32)], -1)
    return mc, ms, dc, ds


def kernel(x_prompt, x_sample, cache_mla_ckv, cache_mla_kpe, cache_diff_k, cache_diff_v, state_gla, meta_tokens, w_in_attn, mla_q_norm, mla_w_uq, mla_kv_norm, mla_w_uk, mla_w_uv, diff_lambda, diff_subln, w_out_attn, w_in_gla, gla_w_gate, gla_b_gate, gla_norm, w_out_gla, ln1_g, ln1_b, ln2_g, ln2_b, moe_w_router, moe_b_router, moe_w_gu, moe_b_gu, moe_w_down, moe_b_down):
    bp, seq, d = x_prompt.shape
    bs, ts, _ = x_sample.shape
    n_meta = meta_tokens.shape[0]
    past_len = cache_mla_ckv.shape[2]
    depth = ln1_g.shape[0]
    assert depth == 2 and ts == n_meta and seq % CHUNK == 0
    alpha = (2 * depth) ** 0.25

    n_heads = mla_w_uq.shape[2]
    q_lora = mla_w_uq.shape[1]
    kv_lora = mla_w_uk.shape[1]
    nope = mla_w_uk.shape[3]
    rope = mla_w_uq.shape[3] - nope
    vdim = mla_w_uv.shape[3]
    diff_heads = cache_diff_k.shape[3]
    diff_qk = cache_diff_k.shape[5]
    diff_cols = diff_heads * 2 * diff_qk
    assert nope == LANES and 2 * rope == LANES and diff_qk == LANES and n_heads % 2 == 0
    assert cache_diff_v.shape[4] == 2 * diff_qk and kv_lora % LANES == 0 and q_lora % LANES == 0

    n_f = bp * seq
    row_meta = n_f
    row_samp = n_f + bp * n_meta
    x = jnp.concatenate([x_prompt.reshape(n_f, d),
                         jnp.broadcast_to(meta_tokens.astype(f32)[None], (bp, n_meta, d)).reshape(bp * n_meta, d),
                         x_sample.reshape(bs * ts, d)], 0)
    pos = jnp.concatenate([jnp.tile(n_meta + jnp.arange(seq, dtype=i32), bp),
                           jnp.tile(jnp.arange(n_meta, dtype=i32), bp),
                           jnp.tile(past_len + jnp.arange(ts, dtype=i32), bs)])
    tq = _tile(seq, (512, 256, 128, 64))

    w_gu_b, w_down_b = moe_w_gu.astype(bf16), moe_w_down.astype(bf16)

    def moe(x, xt, layer, **split):
        return _moe_ln(x, xt, layer, moe_w_router[layer], moe_b_router[layer], w_gu_b, moe_b_gu, w_down_b,
                       moe_b_down, ln2_g[layer][None], ln2_b[layer][None], alpha=alpha, **split)

    w_in = w_in_attn[0]
    s1 = q_lora + kv_lora
    s2 = s1 + rope
    w_in_perm = jnp.concatenate([w_in[:, :s1], w_in[:, s2:], w_in[:, s1:s2],
                                 jnp.zeros((d, LANES - rope), f32)], 1).astype(bf16)
    y = _matmul(x, w_in_perm, f32, "attn_in_proj")
    wuq = mla_w_uq[0]
    wuq_perm = jnp.concatenate([wuq[:, :, :nope].reshape(q_lora, n_heads * nope),
                                wuq[:, :, nope:].reshape(q_lora, n_heads * rope)], 1).astype(bf16)
    wuk_t = jnp.transpose(mla_w_uk[0], (1, 2, 0)).astype(bf16)
    wuv = jnp.transpose(mla_w_uv[0], (1, 0, 2)).astype(bf16)
    tabs = _rope_tables(pos, rope)
    qm, ckv, kpe, km, dqb, dk, dkb, dv, dvb = _attn_prep(
        y, tabs, mla_q_norm[0][None], mla_kv_norm[0][None], wuq_perm, wuk_t,
        n_heads=n_heads, q_lora=q_lora, kv_lora=kv_lora, nope=nope, rope=rope, diff_cols=diff_cols,
        diff_heads=diff_heads)

    mla_scale = (nope + rope) ** -0.5
    diff_scale = diff_qk ** -0.5
    lambda_init = 0.8 - 0.6 * math.exp(-0.3 * 0)
    lam_vecs = diff_lambda[0].astype(f32)
    subln = diff_subln[0][None]
    common_d = dict(scale=diff_scale, lambda_init=lambda_init)
    o_mla = jnp.concatenate([
        _mla_flash(qm, km, wuv, n_batch=bp, seq=seq, n_meta=n_meta, scale=mla_scale, kv_lora=kv_lora, tq=tq),
        _mla_small(qm, km, wuv, None, n_seq=bp, tq=n_meta, row0=row_meta, scale=mla_scale, kv_lora=kv_lora),
        _mla_small(qm, km, wuv, (cache_mla_ckv[0], cache_mla_kpe[0]), n_seq=bs, tq=ts, row0=row_samp,
                   scale=mla_scale, kv_lora=kv_lora)], 0)
    past_d = (cache_diff_k[0].reshape(bs, past_len, diff_cols), cache_diff_v[0].reshape(bs, past_len, diff_cols))
    o_dif = jnp.concatenate([
        _diff_flash(dqb, dkb, dvb, lam_vecs, subln, n_batch=bp, seq=seq, n_meta=n_meta, tq=tq, **common_d),
        _diff_small(dqb, dkb, dvb, lam_vecs, subln, None, n_seq=bp, tq=n_meta, row0=row_meta, **common_d),
        _diff_small(dqb, dkb, dvb, lam_vecs, subln, past_d, n_seq=bs, tq=ts, row0=row_samp, **common_d)], 0)
    w_out = w_out_attn[0].astype(bf16)
    mla_out = n_heads * vdim
    x, xt = _proj_ln([o_mla, o_dif], [w_out[:mla_out], w_out[mla_out:]], x, ln1_g[0][None], ln1_b[0][None],
                     alpha=alpha)
    x = moe(x, xt, 0)

    gla_heads = state_gla.shape[2]
    gla_dk, gla_dv = state_gla.shape[3], state_gla.shape[4]
    qk_cols, v_cols = gla_heads * gla_dk, gla_heads * gla_dv
    rank = gla_w_gate.shape[1]
    w_gla = jnp.concatenate([w_in_gla[0], jnp.zeros((d, LANES - rank), f32)], 1).astype(bf16)
    yg = _matmul(x, w_gla, f32, "gla_in_proj")
    w_gate = jnp.concatenate([gla_w_gate[0], jnp.zeros((LANES - rank, qk_cols), f32)], 0)
    b_gate = gla_b_gate[0][None]
    g_norm = gla_norm[0][None]
    gla_kw = dict(n_heads=gla_heads, qk_cols=qk_cols, v_cols=v_cols)
    s_init = jnp.concatenate([jnp.zeros((bp,) + state_gla.shape[2:], f32), state_gla[0].astype(f32)], 0)
    o_short, s_short = _gla(yg, w_gate, b_gate, g_norm, s_init, n_seq=bp + bs, n_chunks=1, chunk=n_meta,
                            row0=row_meta, **gla_kw)
    o_frames, s_frames = _gla(yg, w_gate, b_gate, g_norm, s_short[:bp], n_seq=bp, n_chunks=seq // GLA_CHUNK,
                              chunk=GLA_CHUNK, row0=0, **gla_kw)
    o_gla = jnp.concatenate([o_frames, o_short], 0)
    x, xt = _proj_ln([o_gla], [w_out_gla[0].astype(bf16)], x, ln1_g[1][None], ln1_b[1][None], alpha=alpha)
    y_frames, y_samp = moe(x, xt, 1, head_rows=n_f, tail_row0=row_samp)

    def rows_p(a, tail):
        c = a.shape[1]
        return jnp.concatenate([a[row_meta:row_samp].reshape(bp, n_meta, c), a[:n_f].reshape(bp, seq, c)],
                               1).reshape((1, bp, n_meta + seq) + tail)

    def rows_s(a, tail):
        return a[row_samp:].reshape((1, bs, ts) + tail)

    kpe = kpe[:, :rope]
    t_dk, t_dv = (diff_heads, 2, diff_qk), (diff_heads, 2 * diff_qk)
    return (y_frames.reshape(bp, seq, d), y_samp.reshape(bs, ts, d),
            rows_p(ckv, (kv_lora,)), rows_p(kpe, (rope,)), rows_p(dk, t_dk), rows_p(dv, t_dv), s_frames[None],
            rows_s(ckv, (kv_lora,)), rows_s(kpe, (rope,)), rows_s(dk, t_dk), rows_s(dv, t_dv), s_short[bp:][None])
```

```python
import functools
import math

import jax
import jax.numpy as jnp
from jax import lax
from jax.experimental import pallas as pl
from jax.experimental.pallas import tpu as pltpu

f32 = jnp.float32
bf16 = jnp.bfloat16
i32 = jnp.int32

CHUNK = 64
ROPE_THETA = 500000.0
GLA_TAU = 16.0
GLA_CHUNK = 64
TOP_K = 4
SWIGLU_LIMIT = 7.0
SWIGLU_ALPHA = 1.702
DIFF_ROT = 32
NEG_BIG = -1e30
ROW_DMA_UNROLL = 8

LANES = 128
VMEM_LIMIT_BYTES = 56 * 1024 * 1024
EXPERT_TILE_BYTES = 4 * 1024 * 1024


def _params(*sem):
    return pltpu.CompilerParams(dimension_semantics=sem, vmem_limit_bytes=VMEM_LIMIT_BYTES)


def _tile(n, prefs):
    for t in prefs:
        if n % t == 0:
            return t
    raise ValueError(f"no tile in {prefs} divides {n}")


def _dot(a, b, precision=None):
    return jnp.dot(a, b, preferred_element_type=f32, precision=precision)


def _dot_nt(a, b, precision=None):
    return lax.dot_general(a, b, (((1,), (1,)), ((), ())), preferred_element_type=f32, precision=precision)


def _rms(x, g, eps=1e-6):
    return x * lax.rsqrt(jnp.mean(x * x, -1, keepdims=True) + eps) * g


def _layer_norm(x, g, b, eps=1e-5):
    mu = jnp.mean(x, -1, keepdims=True)
    xc = x - mu
    var = jnp.mean(xc * xc, -1, keepdims=True)
    return xc * lax.rsqrt(var + eps) * g + b


def _to_token_major(value):
    tm, d = value.shape
    groups = d // LANES
    parts = jnp.stack([value[:, s * LANES:(s + 1) * LANES] for s in range(groups)], 0)
    return jnp.swapaxes(parts, 0, 1).reshape(tm * groups, LANES)


def _from_token_major(value, tm):
    groups = value.shape[0] // tm
    return jnp.swapaxes(value.reshape(tm, groups, LANES), 0, 1)


def _mm_body(x_ref, w_ref, o_ref):
    o_ref[...] = _dot(x_ref[...].astype(bf16), w_ref[...].astype(bf16)).astype(o_ref.dtype)


def _matmul(x, w, out_dtype, name):
    m, k = x.shape
    n = w.shape[1]
    tm = _tile(m, (512, 256, 128, 64, 32, 16, 8))
    tn = _tile(n, (1408, 1024, 896, 768, 640, 512, 384, 256, 128))
    return pl.pallas_call(
        _mm_body,
        grid=(m // tm, n // tn),
        in_specs=[pl.BlockSpec((tm, k), lambda i, j: (i, 0)),
                  pl.BlockSpec((k, tn), lambda i, j: (0, j))],
        out_specs=pl.BlockSpec((tm, tn), lambda i, j: (i, j)),
        out_shape=jax.ShapeDtypeStruct((m, n), out_dtype),
        compiler_params=_params("parallel", "parallel"),
        name=name,
    )(x, w)


def _attn_prep_body(y_ref, mc_ref, ms_ref, dc_ref, ds_ref, qn_ref, kvn_ref, wuq_ref, wuk_ref, wuk2_ref,
                    qm_ref, ckv_ref, kpe_ref, km_ref, dqb_ref, dk_ref, dkb_ref, dv_ref, dvb_ref,
                    qt_ref, kvt_ref, dqt_ref, dvt_ref,
                    *, n_heads, q_lora, kv_lora, nope, rope, diff_cols):
    tm = y_ref.shape[0]
    lane = lax.broadcasted_iota(i32, (tm, LANES), 1)
    mc, ms, dc, ds = mc_ref[...], ms_ref[...], dc_ref[...], ds_ref[...]
    half = rope // 2

    def rope_pair(v):
        rolled = jnp.where((lane % rope) < half, pltpu.roll(v, LANES - half, 1), pltpu.roll(v, half, 1))
        return v * mc + rolled * ms

    dhalf = DIFF_ROT // 2

    def rope_diff(v):
        rolled = jnp.where(lane < dhalf, pltpu.roll(v, LANES - dhalf, 1), pltpu.roll(v, dhalf, 1))
        return v * dc + rolled * ds

    qn = _rms(y_ref[:, 0:q_lora], qn_ref[...]).astype(bf16)
    q = _dot(qn, wuq_ref[...])
    for h in range(n_heads):
        qh = q[:, h * nope:(h + 1) * nope].astype(bf16)
        qm_ref[h, :, 0:kv_lora] = _dot(qh, wuk_ref[h]).astype(bf16)
        qt_ref[h, 0:kv_lora, :] = _dot_nt(wuk2_ref[h], qh).astype(bf16)
    base = n_heads * nope
    for g in range(n_heads // 2):
        pe32 = rope_pair(q[:, base + g * LANES: base + (g + 1) * LANES])
        pe = pe32.astype(bf16)
        qm_ref[2 * g, :, kv_lora:kv_lora + rope] = pe[:, 0:rope]
        qm_ref[2 * g + 1, :, kv_lora:kv_lora + rope] = pe[:, rope:2 * rope]
        pe_t = pe32.T.astype(bf16)
        qt_ref[2 * g, kv_lora:kv_lora + rope, :] = pe_t[0:rope]
        qt_ref[2 * g + 1, kv_lora:kv_lora + rope, :] = pe_t[rope:2 * rope]
    ckv = _rms(y_ref[:, q_lora:q_lora + kv_lora], kvn_ref[...])
    ckv_ref[...] = ckv
    km_ref[:, 0:kv_lora] = ckv.astype(bf16)
    kvt_ref[...] = ckv.T.astype(bf16)
    o_dq = q_lora + kv_lora
    o_dk = o_dq + diff_cols
    o_dv = o_dk + diff_cols
    o_pe = o_dv + diff_cols
    kpe = rope_pair(y_ref[:, o_pe:o_pe + LANES])
    kpe_ref[...] = kpe
    km_ref[:, kv_lora:kv_lora + rope] = kpe[:, 0:rope].astype(bf16)
    for g in range(diff_cols // LANES):
        dq = rope_diff(y_ref[:, o_dq + g * LANES:o_dq + (g + 1) * LANES])
        dqb_ref[g] = dq.astype(bf16)
        dqt_ref[g] = dq.T.astype(bf16)
        dk = rope_diff(y_ref[:, o_dk + g * LANES:o_dk + (g + 1) * LANES])
        dk_ref[:, g * LANES:(g + 1) * LANES] = dk
        dkb_ref[g] = dk.astype(bf16)
    dv = y_ref[:, o_dv:o_dv + diff_cols]
    dv_ref[...] = dv
    vw = dvb_ref.shape[2]
    for h in range(dvb_ref.shape[0]):
        dvb_ref[h] = dv[:, h * vw:(h + 1) * vw].astype(bf16)
        dvt_ref[h] = dv[:, h * vw:(h + 1) * vw].T.astype(bf16)


def _attn_prep(y, tabs, q_norm, kv_norm, wuq, wukT, wuk2, *, n_heads, q_lora, kv_lora, nope, rope, diff_cols,
               diff_heads):
    r, ycols = y.shape
    tm = _tile(r, (256, 128))
    dm = kv_lora + rope
    n_qk = diff_cols // LANES
    vw = diff_cols // diff_heads
    row = lambda i: (i, 0)
    mid = lambda i: (0, i, 0)
    last = lambda i: (0, 0, i)
    const2 = lambda i: (0, 0)
    body = functools.partial(_attn_prep_body, n_heads=n_heads, q_lora=q_lora, kv_lora=kv_lora,
                             nope=nope, rope=rope, diff_cols=diff_cols)
    return pl.pallas_call(
        body,
        grid=(r // tm,),
        in_specs=[pl.BlockSpec((tm, ycols), row)]
                 + [pl.BlockSpec((tm, LANES), row)] * 4
                 + [pl.BlockSpec((1, q_lora), const2), pl.BlockSpec((1, kv_lora), const2),
                    pl.BlockSpec(wuq.shape, const2), pl.BlockSpec(wukT.shape, lambda i: (0, 0, 0)),
                    pl.BlockSpec(wuk2.shape, lambda i: (0, 0, 0))],
        out_specs=[pl.BlockSpec((n_heads, tm, dm), mid),
                   pl.BlockSpec((tm, kv_lora), row), pl.BlockSpec((tm, LANES), row),
                   pl.BlockSpec((tm, dm), row), pl.BlockSpec((n_qk, tm, LANES), mid),
                   pl.BlockSpec((tm, diff_cols), row), pl.BlockSpec((n_qk, tm, LANES), mid),
                   pl.BlockSpec((tm, diff_cols), row), pl.BlockSpec((diff_heads, tm, vw), mid),
                   pl.BlockSpec((n_heads, dm, tm), last), pl.BlockSpec((kv_lora, tm), lambda i: (0, i)),
                   pl.BlockSpec((n_qk, LANES, tm), last), pl.BlockSpec((diff_heads, vw, tm), last)],
        out_shape=[jax.ShapeDtypeStruct((n_heads, r, dm), bf16),
                   jax.ShapeDtypeStruct((r, kv_lora), f32), jax.ShapeDtypeStruct((r, LANES), f32),
                   jax.ShapeDtypeStruct((r, dm), bf16), jax.ShapeDtypeStruct((n_qk, r, LANES), bf16),
                   jax.ShapeDtypeStruct((r, diff_cols), f32), jax.ShapeDtypeStruct((n_qk, r, LANES), bf16),
                   jax.ShapeDtypeStruct((r, diff_cols), f32), jax.ShapeDtypeStruct((diff_heads, r, vw), bf16),
                   jax.ShapeDtypeStruct((n_heads, dm, r), bf16), jax.ShapeDtypeStruct((kv_lora, r), bf16),
                   jax.ShapeDtypeStruct((n_qk, LANES, r), bf16), jax.ShapeDtypeStruct((diff_heads, vw, r), bf16)],
        compiler_params=_params("parallel"),
        name="attn_prep",
    )(y, *tabs, q_norm, kv_norm, wuq, wukT, wuk2)


def _chunk_mask(tk, tq):
    keys = lax.broadcasted_iota(i32, (tk, tq), 0)
    queries = lax.broadcasted_iota(i32, (tk, tq), 1)
    return (keys // CHUNK) <= (queries // CHUNK)


def _meta_mask(tk, tq, first, count):
    keys = lax.broadcasted_iota(i32, (tk, tq), 0)
    return jnp.logical_and(keys >= first, keys < first + count)


def _softmax_step(s, v_t, m_ref, l_ref, acc_ref, c):
    m_prev = m_ref[c]
    m_new = jnp.maximum(m_prev, jnp.max(s, 0, keepdims=True))
    a = jnp.exp(m_prev - m_new)
    p = jnp.exp(s - m_new)
    l_ref[c] = a * l_ref[c] + jnp.sum(p, 0, keepdims=True)
    acc_ref[c] = a * acc_ref[c] + _dot(v_t, p.astype(bf16))
    m_ref[c] = m_new


def _init_softmax(m_ref, l_ref, acc_ref):
    m_ref[...] = jnp.full(m_ref.shape, NEG_BIG, f32)
    l_ref[...] = jnp.zeros(l_ref.shape, f32)
    acc_ref[...] = jnp.zeros(acc_ref.shape, f32)


def _mla_flash_body(q_ref, kf_ref, vf_ref, kmeta_ref, vmeta_ref, wuv_ref, o_ref, m_ref, l_ref, acc_ref,
                    *, scale, n_meta):
    n_heads, _, tq = q_ref.shape
    tk = kf_ref.shape[0]
    vdim = wuv_ref.shape[2]
    qi, kj = pl.program_id(1), pl.program_id(2)

    def attend(k_ref, v_ref, mask):
        def head(h, carry):
            s = _dot(k_ref[...], q_ref[h]) * scale
            if mask is not None:
                s = jnp.where(mask, s, NEG_BIG)
            _softmax_step(s, v_ref[...], m_ref, l_ref, acc_ref, h)
            return carry

        lax.fori_loop(0, n_heads, head, 0, unroll=4)

    @pl.when(kj == 0)
    def _():
        _init_softmax(m_ref, l_ref, acc_ref)
        first = lax.rem(pl.program_id(0) * n_meta, kmeta_ref.shape[0])
        attend(kmeta_ref, vmeta_ref, _meta_mask(kmeta_ref.shape[0], tq, first, n_meta))

    @pl.when(kj < qi)
    def _():
        attend(kf_ref, vf_ref, None)

    @pl.when(kj == qi)
    def _():
        attend(kf_ref, vf_ref, _chunk_mask(tk, tq))

    @pl.when(kj == pl.num_programs(2) - 1)
    def _():
        for h in range(n_heads):
            o = (acc_ref[h] / l_ref[h]).T.astype(bf16)
            o_ref[:, h * vdim:(h + 1) * vdim] = _dot(o, wuv_ref[h]).astype(o_ref.dtype)


def _mla_flash(qt, km, kvt, wuv, *, n_batch, seq, n_meta, scale, tq):
    n_heads, dm, r = qt.shape
    kv_lora = kvt.shape[0]
    vdim = wuv.shape[2]
    nq = seq // tq
    mblk = LANES
    meta_of = lambda b: (n_batch * seq + b * n_meta) // mblk
    body = functools.partial(_mla_flash_body, scale=scale, n_meta=n_meta)
    return pl.pallas_call(
        body,
        grid=(n_batch, nq, nq),
        in_specs=[pl.BlockSpec((n_heads, dm, tq), lambda b, i, j: (0, 0, b * nq + i)),
                  pl.BlockSpec((tq, dm), lambda b, i, j: (b * nq + jnp.minimum(i, j), 0)),
                  pl.BlockSpec((kv_lora, tq), lambda b, i, j: (0, b * nq + jnp.minimum(i, j))),
                  pl.BlockSpec((mblk, dm), lambda b, i, j: (meta_of(b), 0)),
                  pl.BlockSpec((kv_lora, mblk), lambda b, i, j: (0, meta_of(b))),
                  pl.BlockSpec(wuv.shape, lambda b, i, j: (0, 0, 0))],
        out_specs=pl.BlockSpec((tq, n_heads * vdim), lambda b, i, j: (b * nq + i, 0)),
        out_shape=jax.ShapeDtypeStruct((n_batch * seq, n_heads * vdim), bf16),
        scratch_shapes=[pltpu.VMEM((n_heads, 1, tq), f32), pltpu.VMEM((n_heads, 1, tq), f32),
                        pltpu.VMEM((n_heads, kv_lora, tq), f32)],
        compiler_params=_params("parallel", "parallel", "arbitrary"),
        name="mla_flash",
    )(qt, km, kvt, km, kvt, wuv)


def _diff_lambda(lam_ref, lambda_init):
    lv = lam_ref[...]
    return (jnp.exp(jnp.sum(lv[0:1] * lv[1:2], keepdims=True))
            - jnp.exp(jnp.sum(lv[2:3] * lv[3:4], keepdims=True)) + lambda_init)


def _diff_flash_body(q_ref, kf_ref, vf_ref, kmeta_ref, vmeta_ref, lam_ref, sub_ref, o_ref,
                     m_ref, l_ref, acc_ref, *, scale, lambda_init, n_meta):
    n_qk, _, tq = q_ref.shape
    tk = kf_ref.shape[1]
    vw = vf_ref.shape[1]
    qi, kj = pl.program_id(1), pl.program_id(2)

    def attend(k_ref, v_ref, mask):
        def part(c, carry):
            s = _dot(k_ref[c], q_ref[c]) * scale
            if mask is not None:
                s = jnp.where(mask, s, NEG_BIG)
            _softmax_step(s, v_ref[c // 2], m_ref, l_ref, acc_ref, c)
            return carry

        lax.fori_loop(0, n_qk, part, 0, unroll=4)

    @pl.when(kj == 0)
    def _():
        _init_softmax(m_ref, l_ref, acc_ref)
        first = lax.rem(pl.program_id(0) * n_meta, kmeta_ref.shape[1])
        attend(kmeta_ref, vmeta_ref, _meta_mask(kmeta_ref.shape[1], tq, first, n_meta))

    @pl.when(kj < qi)
    def _():
        attend(kf_ref, vf_ref, None)

    @pl.when(kj == qi)
    def _():
        attend(kf_ref, vf_ref, _chunk_mask(tk, tq))

    @pl.when(kj == pl.num_programs(2) - 1)
    def _():
        lam = _diff_lambda(lam_ref, lambda_init)
        for h in range(n_qk // 2):
            o = (acc_ref[2 * h] / l_ref[2 * h] - lam * (acc_ref[2 * h + 1] / l_ref[2 * h + 1])).T
            o_ref[:, h * vw:(h + 1) * vw] = (_rms(o, sub_ref[...]) * (1.0 - lambda_init)).astype(o_ref.dtype)


def _diff_flash(dqt, dkb, dvt, lam_vecs, subln, *, n_batch, seq, n_meta, scale, lambda_init, tq):
    n_qk, d, r = dqt.shape
    n_heads, vw, _ = dvt.shape
    nq = seq // tq
    mblk = LANES
    meta_of = lambda b: (n_batch * seq + b * n_meta) // mblk
    body = functools.partial(_diff_flash_body, scale=scale, lambda_init=lambda_init, n_meta=n_meta)
    kblk = lambda b, i, j: b * nq + jnp.minimum(i, j)
    const2 = lambda b, i, j: (0, 0)
    return pl.pallas_call(
        body,
        grid=(n_batch, nq, nq),
        in_specs=[pl.BlockSpec((n_qk, d, tq), lambda b, i, j: (0, 0, b * nq + i)),
                  pl.BlockSpec((n_qk, tq, d), lambda b, i, j: (0, kblk(b, i, j), 0)),
                  pl.BlockSpec((n_heads, vw, tq), lambda b, i, j: (0, 0, kblk(b, i, j))),
                  pl.BlockSpec((n_qk, mblk, d), lambda b, i, j: (0, meta_of(b), 0)),
                  pl.BlockSpec((n_heads, vw, mblk), lambda b, i, j: (0, 0, meta_of(b))),
                  pl.BlockSpec(lam_vecs.shape, const2), pl.BlockSpec(subln.shape, const2)],
        out_specs=pl.BlockSpec((tq, n_heads * vw), lambda b, i, j: (b * nq + i, 0)),
        out_shape=jax.ShapeDtypeStruct((n_batch * seq, n_heads * vw), bf16),
        scratch_shapes=[pltpu.VMEM((n_qk, 1, tq), f32), pltpu.VMEM((n_qk, 1, tq), f32),
                        pltpu.VMEM((n_qk, vw, tq), f32)],
        compiler_params=_params("parallel", "parallel", "arbitrary"),
        name="diff_flash",
    )(dqt, dkb, dvt, dkb, dvt, lam_vecs, subln)


def _mla_small_body(*refs, scale, kv_lora, has_past):
    if has_past:
        q_ref, kown_ref, ckvp_ref, kpep_ref, wuv_ref, o_ref = refs
    else:
        q_ref, kown_ref, wuv_ref, o_ref = refs
    n_heads, tq, dm = q_ref.shape
    vdim = wuv_ref.shape[2]
    q = q_ref[...].reshape(n_heads * tq, dm)
    kown = kown_ref[...]
    s_own = _dot_nt(q, kown) * scale
    m = jnp.max(s_own, -1, keepdims=True)
    if has_past:
        kp = ckvp_ref[0].astype(bf16)
        s_past = (_dot_nt(q[:, 0:kv_lora], kp) + _dot_nt(q[:, kv_lora:dm], kpep_ref[0].astype(bf16))) * scale
        m = jnp.maximum(m, jnp.max(s_past, -1, keepdims=True))
    p_own = jnp.exp(s_own - m)
    l = jnp.sum(p_own, -1, keepdims=True)
    acc = _dot(p_own.astype(bf16), kown[:, 0:kv_lora])
    if has_past:
        p_past = jnp.exp(s_past - m)
        l = l + jnp.sum(p_past, -1, keepdims=True)
        acc = acc + _dot(p_past.astype(bf16), kp)
    o = (acc / l).astype(bf16)
    for h in range(n_heads):
        o_ref[:, h * vdim:(h + 1) * vdim] = _dot(o[h * tq:(h + 1) * tq], wuv_ref[h]).astype(o_ref.dtype)


def _mla_small(qm, km, wuv, past, *, n_seq, tq, row0, scale, kv_lora):
    n_heads, r, dm = qm.shape
    vdim = wuv.shape[2]
    blk0 = row0 // tq
    has_past = past is not None
    in_specs = [pl.BlockSpec((n_heads, tq, dm), lambda b: (0, blk0 + b, 0)),
                pl.BlockSpec((tq, dm), lambda b: (blk0 + b, 0))]
    args = [qm, km]
    if has_past:
        ckvp, kpep = past
        in_specs += [pl.BlockSpec((1,) + ckvp.shape[1:], lambda b: (b, 0, 0)),
                     pl.BlockSpec((1,) + kpep.shape[1:], lambda b: (b, 0, 0))]
        args += [ckvp, kpep]
    in_specs.append(pl.BlockSpec(wuv.shape, lambda b: (0, 0, 0)))
    args.append(wuv)
    body = functools.partial(_mla_small_body, scale=scale, kv_lora=kv_lora, has_past=has_past)
    return pl.pallas_call(
        body,
        grid=(n_seq,),
        in_specs=in_specs,
        out_specs=pl.BlockSpec((tq, n_heads * vdim), lambda b: (b, 0)),
        out_shape=jax.ShapeDtypeStruct((n_seq * tq, n_heads * vdim), bf16),
        compiler_params=_params("parallel"),
        name="mla_small_past" if has_past else "mla_small",
    )(*args)


def _diff_small_body(*refs, scale, lambda_init, has_past):
    if has_past:
        q_ref, kown_ref, vown_ref, kp_ref, vp_ref, lam_ref, sub_ref, o_ref = refs
    else:
        q_ref, kown_ref, vown_ref, lam_ref, sub_ref, o_ref = refs
    d = q_ref.shape[2]
    outs = []
    for i in range(2):
        q = q_ref[i]
        s_own = _dot_nt(q, kown_ref[i]) * scale
        m = jnp.max(s_own, -1, keepdims=True)
        if has_past:
            s_past = _dot_nt(q, kp_ref[0, :, i * d:(i + 1) * d].astype(bf16)) * scale
            m = jnp.maximum(m, jnp.max(s_past, -1, keepdims=True))
        p_own = jnp.exp(s_own - m)
        l = jnp.sum(p_own, -1, keepdims=True)
        acc = _dot(p_own.astype(bf16), vown_ref[0])
        if has_past:
            p_past = jnp.exp(s_past - m)
            l = l + jnp.sum(p_past, -1, keepdims=True)
            acc = acc + _dot(p_past.astype(bf16), vp_ref[0].astype(bf16))
        outs.append(acc / l)
    o = outs[0] - _diff_lambda(lam_ref, lambda_init) * outs[1]
    o_ref[...] = (_rms(o, sub_ref[...]) * (1.0 - lambda_init)).astype(o_ref.dtype)


def _diff_small(dqb, dkb, dvb, lam_vecs, subln, past, *, n_seq, tq, row0, scale, lambda_init):
    n_qk, r, d = dqb.shape
    n_heads, _, vw = dvb.shape
    blk0 = row0 // tq
    has_past = past is not None
    own = lambda b, h: (h, blk0 + b, 0)
    in_specs = [pl.BlockSpec((2, tq, d), own), pl.BlockSpec((2, tq, d), own), pl.BlockSpec((1, tq, vw), own)]
    args = [dqb, dkb, dvb]
    if has_past:
        kp, vp = past
        in_specs += [pl.BlockSpec((1, kp.shape[1], vw), lambda b, h: (b, 0, h)),
                     pl.BlockSpec((1, vp.shape[1], vw), lambda b, h: (b, 0, h))]
        args += [kp, vp]
    in_specs += [pl.BlockSpec(lam_vecs.shape, lambda b, h: (0, 0)), pl.BlockSpec(subln.shape, lambda b, h: (0, 0))]
    args += [lam_vecs, subln]
    body = functools.partial(_diff_small_body, scale=scale, lambda_init=lambda_init, has_past=has_past)
    return pl.pallas_call(
        body,
        grid=(n_seq, n_heads),
        in_specs=in_specs,
        out_specs=pl.BlockSpec((tq, vw), lambda b, h: (b, h)),
        out_shape=jax.ShapeDtypeStruct((n_seq * tq, n_heads * vw), bf16),
        compiler_params=_params("parallel", "parallel"),
        name="diff_small_past" if has_past else "diff_small",
    )(*args)


def _gla_body(q_ref, k_ref, v_ref, r_ref, g_ref, wg_ref, bg_ref, gn_ref, s0_ref, o_ref, sout_ref, s_ref,
              *, n_heads):
    c = q_ref.shape[0]
    dk = q_ref.shape[1] // n_heads
    dv = v_ref.shape[1] // n_heads
    step = pl.program_id(1)

    @pl.when(step == 0)
    def _():
        s_ref[...] = s0_ref[0]

    pre = _dot(g_ref[...], wg_ref[...], precision=lax.Precision.HIGHEST) + bg_ref[...]
    log_a = jax.nn.log_sigmoid(pre) / GLA_TAU
    rows = lax.broadcasted_iota(i32, (c, c), 0)
    cols = lax.broadcasted_iota(i32, (c, c), 1)
    causal = rows >= cols
    tril = causal.astype(f32)
    for h in range(n_heads):
        la = log_a[:, h * dk:(h + 1) * dk]
        b = _dot(tril, la, precision=lax.Precision.HIGHEST)
        b_t = b.T
        b_last = b_t[:, c - 1:c]
        qh = q_ref[:, h * dk:(h + 1) * dk] * (dk ** -0.5)
        kh = k_ref[:, h * dk:(h + 1) * dk]
        vh = v_ref[:, h * dv:(h + 1) * dv].astype(bf16)
        q_in = (qh * jnp.exp(b)).astype(bf16)
        k_out = (kh * jnp.exp(-b)).astype(bf16)
        scores = jnp.where(causal, _dot_nt(q_in, k_out), 0.0)
        s_prev = s_ref[h]
        o = _dot(scores.astype(bf16), vh) + _dot(q_in, s_prev.astype(bf16))
        k_dec_t = (kh.T * jnp.exp(b_last - b_t)).astype(bf16)
        s_ref[h] = jnp.exp(b_last) * s_prev + _dot(k_dec_t, vh)
        gate = jax.nn.silu(r_ref[:, h * dv:(h + 1) * dv])
        o_ref[:, h * dv:(h + 1) * dv] = (_rms(o, gn_ref[...]) * gate).astype(o_ref.dtype)

    @pl.when(step == pl.num_programs(1) - 1)
    def _():
        sout_ref[0] = s_ref[...]


def _gla(y, w_gate, b_gate, g_norm, s0, *, n_seq, n_chunks, chunk, row0, n_heads, qk_cols, v_cols):
    blk0 = row0 // chunk
    rmap = lambda col: (lambda b, s: (blk0 + b * n_chunks + s, col))
    g_col = (2 * qk_cols + 2 * v_cols) // LANES
    const2 = lambda b, s: (0, 0)
    smap = lambda b, s: (b, 0, 0, 0)
    body = functools.partial(_gla_body, n_heads=n_heads)
    return pl.pallas_call(
        body,
        grid=(n_seq, n_chunks),
        in_specs=[pl.BlockSpec((chunk, qk_cols), rmap(0)), pl.BlockSpec((chunk, qk_cols), rmap(1)),
                  pl.BlockSpec((chunk, v_cols), rmap(2 * qk_cols // v_cols)),
                  pl.BlockSpec((chunk, v_cols), rmap(2 * qk_cols // v_cols + 1)),
                  pl.BlockSpec((chunk, LANES), rmap(g_col)),
                  pl.BlockSpec(w_gate.shape, const2), pl.BlockSpec(b_gate.shape, const2),
                  pl.BlockSpec(g_norm.shape, const2),
                  pl.BlockSpec((1,) + s0.shape[1:], smap)],
        out_specs=[pl.BlockSpec((chunk, v_cols), lambda b, s: (b * n_chunks + s, 0)),
                   pl.BlockSpec((1,) + s0.shape[1:], smap)],
        out_shape=[jax.ShapeDtypeStruct((n_seq * n_chunks * chunk, v_cols), bf16),
                   jax.ShapeDtypeStruct(s0.shape, f32)],
        scratch_shapes=[pltpu.VMEM(s0.shape[1:], f32)],
        compiler_params=_params("parallel", "arbitrary"),
        name=f"gla_chunk{chunk}",
    )(y, y, y, y, y, w_gate, b_gate, g_norm, s0)


def _proj_ln_body(*refs, n_in, alpha):
    a_refs = refs[:n_in]
    w_refs = refs[n_in:2 * n_in]
    x_ref, g_ref, b_ref, o_ref, ot_ref = refs[2 * n_in:]
    h = _dot(a_refs[0][...], w_refs[0][...])
    for a_ref, w_ref in zip(a_refs[1:], w_refs[1:]):
        h = h + _dot(a_ref[...], w_ref[...])
    out = _layer_norm(alpha * x_ref[...] + h, g_ref[...], b_ref[...])
    o_ref[...] = out
    ot_ref[...] = _to_token_major(out)


def _proj_ln(acts, weights, x, g, b, *, alpha):
    r, d = x.shape
    groups = d // LANES
    tm = _tile(r, (256, 128, 64, 32, 16, 8))
    row = lambda i: (i, 0)
    const2 = lambda i: (0, 0)
    body = functools.partial(_proj_ln_body, n_in=len(acts), alpha=alpha)
    return pl.pallas_call(
        body,
        grid=(r // tm,),
        in_specs=[pl.BlockSpec((tm, a.shape[1]), row) for a in acts]
                 + [pl.BlockSpec(w.shape, const2) for w in weights]
                 + [pl.BlockSpec((tm, d), row), pl.BlockSpec((1, d), const2), pl.BlockSpec((1, d), const2)],
        out_specs=[pl.BlockSpec((tm, d), row), pl.BlockSpec((tm * groups, LANES), row)],
        out_shape=[jax.ShapeDtypeStruct((r, d), f32), jax.ShapeDtypeStruct((r * groups, LANES), f32)],
        compiler_params=_params("parallel"),
        name="proj_ln",
    )(*acts, *weights, x, g, b)


def _router_body(x_ref, w_ref, b_ref, idx_ref, gate_ref, rank_ref, cnt_ref, base_ref):
    n_exp = w_ref.shape[0]
    tm = x_ref.shape[0]

    @pl.when(pl.program_id(0) == 0)
    def _():
        base_ref[...] = jnp.zeros(base_ref.shape, f32)

    logits = _dot_nt(w_ref[...], x_ref[...], precision=lax.Precision.HIGHEST) + b_ref[...]
    eid = lax.broadcasted_iota(i32, (n_exp, tm), 0).astype(f32)
    work = logits
    vals, hots = [], []
    for k in range(TOP_K):
        m = jnp.max(work, 0, keepdims=True)
        sel = jnp.min(jnp.where(work == m, eid, float(n_exp)), 0, keepdims=True)
        hot = eid == sel
        idx_ref[k:k + 1, :] = sel.astype(i32)
        vals.append(m)
        hots.append(hot)
        work = jnp.where(hot, -jnp.inf, work)
    exps = [jnp.exp(v - vals[0]) for v in vals]
    denom = functools.reduce(jnp.add, exps)
    for k in range(TOP_K):
        gate_ref[k:k + 1, :] = exps[k] / denom
    hot_all = functools.reduce(jnp.logical_or, hots).astype(bf16)
    t_row = lax.broadcasted_iota(i32, (tm, tm), 0)
    t_col = lax.broadcasted_iota(i32, (tm, tm), 1)
    before = (t_row < t_col).astype(bf16)
    prefix = _dot(hot_all, before) + base_ref[...]
    for k in range(TOP_K):
        rank_ref[k:k + 1, :] = jnp.sum(jnp.where(hots[k], prefix, 0.0), 0, keepdims=True).astype(i32)
    base_ref[...] = base_ref[...] + jnp.sum(hot_all.astype(f32), 1, keepdims=True)
    cnt_ref[...] = jnp.broadcast_to(base_ref[...], cnt_ref.shape)


def _router(x, w_t, b_col):
    r, d = x.shape
    n_exp = w_t.shape[0]
    tm = _tile(r, (512, 256, 128))
    col = lambda i: (0, i)
    const2 = lambda i: (0, 0)
    return pl.pallas_call(
        _router_body,
        grid=(r // tm,),
        in_specs=[pl.BlockSpec((tm, d), lambda i: (i, 0)), pl.BlockSpec((n_exp, d), const2),
                  pl.BlockSpec((n_exp, 1), const2)],
        out_specs=[pl.BlockSpec((TOP_K, tm), col), pl.BlockSpec((TOP_K, tm), col), pl.BlockSpec((TOP_K, tm), col),
                   pl.BlockSpec((n_exp, LANES), const2)],
        out_shape=[jax.ShapeDtypeStruct((TOP_K, r), i32), jax.ShapeDtypeStruct((TOP_K, r), f32),
                   jax.ShapeDtypeStruct((TOP_K, r), i32), jax.ShapeDtypeStruct((n_exp, LANES), f32)],
        scratch_shapes=[pltpu.VMEM((n_exp, 1), f32)],
        compiler_params=_params("arbitrary"),
        name="router",
    )(x, w_t, b_col)


def _expert_body(be_ref, nused_ref, tok_cur_ref, tok_nxt_ref, dst_prev_ref,
                 x_hbm, wg_ref, wu_ref, bg_ref, bu_ref, wd_ref, bd_ref, ys_hbm,
                 gbuf, xb_ref, acc_ref, obuf, gsem, ssem, *, groups):
    i, f = pl.program_id(0), pl.program_id(1)
    n_f = pl.num_programs(1)
    n_used = nused_ref[0]
    tb = xb_ref.shape[0]
    per_step = tb // n_f
    slot = lax.rem(i, 2)
    other = 1 - slot
    used = i < n_used
    has_prev = jnp.logical_and(i >= 1, i <= n_used)

    def gather_row(tok_ref, j, s):
        src = pl.multiple_of(tok_ref[0, 0, j] * groups, groups)
        dst = pl.multiple_of(j * groups, groups)
        return pltpu.make_async_copy(x_hbm.at[pl.ds(src, groups), :], gbuf.at[s, pl.ds(dst, groups), :],
                                     gsem.at[s])

    def scatter_row(j, s):
        src = pl.multiple_of(j * groups, groups)
        dst = pl.multiple_of(dst_prev_ref[0, 0, j] * groups, groups)
        return pltpu.make_async_copy(obuf.at[s, pl.ds(src, groups), :], ys_hbm.at[pl.ds(dst, groups), :],
                                     ssem.at[s])

    def rows(first, count, fn):
        def step(j, carry):
            fn(first + j)
            return carry
        lax.fori_loop(0, count, step, 0, unroll=ROW_DMA_UNROLL)

    @pl.when(jnp.logical_and(i == 0, f == 0))
    def _():
        rows(0, tb, lambda j: gather_row(tok_cur_ref, j, 0).start())

    @pl.when(i + 1 < n_used)
    def _():
        rows(f * per_step, per_step, lambda j: gather_row(tok_nxt_ref, j, other).start())

    @pl.when(has_prev)
    def _():
        rows(f * per_step, per_step, lambda j: scatter_row(j, other).start())

    @pl.when(jnp.logical_and(used, f == 0))
    def _():
        rows(0, tb, lambda j: gather_row(tok_cur_ref, j, slot).wait())
        g3 = _from_token_major(gbuf[slot], tb)
        for s in range(groups):
            xb_ref[:, s * LANES:(s + 1) * LANES] = g3[s].astype(bf16)

    @pl.when(used)
    def _():
        x = xb_ref[...]
        gate = jnp.minimum(_dot(x, wg_ref[...].astype(bf16)) + bg_ref[...], SWIGLU_LIMIT)
        up = jnp.clip(_dot(x, wu_ref[...].astype(bf16)) + bu_ref[...], -SWIGLU_LIMIT, SWIGLU_LIMIT)
        act = (gate * jax.nn.sigmoid(SWIGLU_ALPHA * gate) * (up + 1.0)).astype(bf16)
        part = _dot(act, wd_ref[...].astype(bf16))

        @pl.when(f == 0)
        def _():
            acc_ref[...] = part

        @pl.when(f > 0)
        def _():
            acc_ref[...] += part

    @pl.when(jnp.logical_and(used, f == n_f - 1))
    def _():
        obuf[slot] = _to_token_major(acc_ref[...] + bd_ref[...])

    @pl.when(jnp.logical_and(has_prev, f == n_f - 1))
    def _():
        rows(0, tb, lambda j: scatter_row(j, other).wait())


def _experts(xt, row_tok, row_dst, block_e, n_used, layer, w_gu, b_gu, w_down, b_down, tb, n_out_rows):
    _, n_exp, d, two_f = w_gu.shape
    groups = d // LANES
    dff = two_f // 2
    tf = _tile(dff, tuple(t for t in (1024, 512, 256, 128) if d * t * w_gu.dtype.itemsize <= EXPERT_TILE_BYTES))
    n_f = dff // tf
    n_blk = row_tok.shape[0] // tb
    assert tb % n_f == 0

    def f_eff(i, f, nused):
        return jnp.where(i < nused[0], f, n_f - 1)

    smem_blk = lambda imap: pl.BlockSpec((1, 1, tb), imap, memory_space=pltpu.SMEM)
    grid_spec = pltpu.PrefetchScalarGridSpec(
        num_scalar_prefetch=2,
        grid=(n_blk, n_f),
        in_specs=[smem_blk(lambda i, f, be, nu: (i, 0, 0)),
                  smem_blk(lambda i, f, be, nu: (jnp.minimum(i + 1, n_blk - 1), 0, 0)),
                  smem_blk(lambda i, f, be, nu: (jnp.maximum(i - 1, 0), 0, 0)),
                  pl.BlockSpec(memory_space=pl.ANY),
                  pl.BlockSpec((None, None, d, tf), lambda i, f, be, nu: (layer, be[i], 0, f_eff(i, f, nu))),
                  pl.BlockSpec((None, None, d, tf), lambda i, f, be, nu: (layer, be[i], 0, n_f + f_eff(i, f, nu))),
                  pl.BlockSpec((None, None, 1, tf), lambda i, f, be, nu: (layer, be[i], 0, f_eff(i, f, nu))),
                  pl.BlockSpec((None, None, 1, tf), lambda i, f, be, nu: (layer, be[i], 0, n_f + f_eff(i, f, nu))),
                  pl.BlockSpec((None, None, tf, d), lambda i, f, be, nu: (layer, be[i], f_eff(i, f, nu), 0)),
                  pl.BlockSpec((None, None, 1, d), lambda i, f, be, nu: (layer, be[i], 0, 0))],
        out_specs=pl.BlockSpec(memory_space=pl.ANY),
        scratch_shapes=[pltpu.VMEM((2, tb * groups, LANES), f32), pltpu.VMEM((tb, d), bf16),
                        pltpu.VMEM((tb, d), f32), pltpu.VMEM((2, tb * groups, LANES), f32),
                        pltpu.SemaphoreType.DMA((2,)), pltpu.SemaphoreType.DMA((2,))],
    )
    n_layers = w_gu.shape[0]
    b_gu4 = b_gu.reshape(n_layers, n_exp, 1, two_f)
    return pl.pallas_call(
        functools.partial(_expert_body, groups=groups),
        grid_spec=grid_spec,
        out_shape=jax.ShapeDtypeStruct((n_out_rows * groups, LANES), f32),
        compiler_params=_params("arbitrary", "arbitrary"),
        name="moe_experts",
    )(block_e, n_used, row_tok.reshape(n_blk, 1, tb), row_tok.reshape(n_blk, 1, tb), row_dst.reshape(n_blk, 1, tb),
      xt, w_gu, w_gu, b_gu4, b_gu4, w_down, b_down.reshape(n_layers, n_exp, 1, d))


def _combine_body(*refs, alpha, groups, head_tiles, tail_tile0):
    y_refs = refs[:TOP_K]
    gate_ref, x_ref, g_ref, b_ref = refs[TOP_K:TOP_K + 4]
    out_refs, y_ref = refs[TOP_K + 4:-1], refs[-1]
    tm = x_ref.shape[0]
    gates = [gate_ref[:, k:k + 1] for k in range(TOP_K)]
    planes = [_from_token_major(y_refs[k][...], tm) for k in range(TOP_K)]
    for s in range(groups):
        acc = planes[0][s] * gates[0]
        for k in range(1, TOP_K):
            acc = acc + planes[k][s] * gates[k]
        y_ref[:, s * LANES:(s + 1) * LANES] = acc
    out = _layer_norm(alpha * x_ref[...] + y_ref[...], g_ref[...], b_ref[...])
    if head_tiles is None:
        out_refs[0][...] = out
    else:
        i = pl.program_id(0)

        @pl.when(i < head_tiles)
        def _():
            out_refs[0][...] = out

        @pl.when(i >= tail_tile0)
        def _():
            out_refs[1][...] = out


def _combine_ln(ys, gates, x, g, b, *, alpha, head_rows=None, tail_row0=None):
    r, d = x.shape
    groups = d // LANES
    tm = _tile(r, (256, 128, 64, 32, 16, 8))
    split = head_rows is not None and head_rows % tm == 0 and tail_row0 % tm == 0
    n_t = r // tm
    row = lambda i: (i, 0)
    const2 = lambda i: (0, 0)
    plane = lambda k: (lambda i: (k * n_t + i, 0))
    if split:
        head_tiles, tail_tile0 = head_rows // tm, tail_row0 // tm
        out_specs = [pl.BlockSpec((tm, d), lambda i: (jnp.minimum(i, head_tiles - 1), 0)),
                     pl.BlockSpec((tm, d), lambda i: (jnp.maximum(i - tail_tile0, 0), 0))]
        out_shape = [jax.ShapeDtypeStruct((head_rows, d), f32), jax.ShapeDtypeStruct((r - tail_row0, d), f32)]
    else:
        head_tiles = tail_tile0 = None
        out_specs = [pl.BlockSpec((tm, d), row)]
        out_shape = [jax.ShapeDtypeStruct((r, d), f32)]
    body = functools.partial(_combine_body, alpha=alpha, groups=groups, head_tiles=head_tiles,
                             tail_tile0=tail_tile0)
    outs = pl.pallas_call(
        body,
        grid=(n_t,),
        in_specs=[pl.BlockSpec((tm * groups, LANES), plane(k)) for k in range(TOP_K)]
                 + [pl.BlockSpec((tm, TOP_K), row), pl.BlockSpec((tm, d), row),
                    pl.BlockSpec((1, d), const2), pl.BlockSpec((1, d), const2)],
        out_specs=out_specs,
        out_shape=out_shape,
        scratch_shapes=[pltpu.VMEM((tm, d), f32)],
        compiler_params=_params("arbitrary"),
        name="moe_combine_ln",
    )(*([ys] * TOP_K), gates, x, g, b)
    if split:
        return outs
    if head_rows is None:
        return outs[0]
    return outs[0][:head_rows], outs[0][tail_row0:]


def _moe_ln(x, xt, layer, w_router, b_router, w_gu, b_gu, w_down, b_down, g, b, *, alpha, **split):
    r, d = x.shape
    n_exp = w_router.shape[1]
    n_asg = r * TOP_K
    tb = 512 if n_asg >= 512 * 8 else 128
    idx_t, gate_t, rank_t, cnt = _router(x, w_router.T, b_router.reshape(n_exp, 1))
    counts = cnt[:, 0].astype(i32)
    padded = (counts + tb - 1) // tb * tb
    pad_end = jnp.cumsum(padded)
    pad_start = pad_end - padded
    hot = idx_t[None] == jnp.arange(n_exp, dtype=i32)[:, None, None]
    pos = rank_t + jnp.sum(jnp.where(hot, pad_start[:, None, None], 0), 0)
    n_blk = -(-n_asg // tb) + n_exp + 2
    n_rows = n_blk * tb
    asg = jnp.arange(n_asg, dtype=i32)
    inv = jnp.full((n_rows,), -1, i32).at[pos.reshape(-1)].set(asg, unique_indices=True)
    real = inv >= 0
    row_tok = jnp.where(real, inv % r, 0)
    row_dst = jnp.where(real, inv, n_asg + jnp.arange(n_rows, dtype=i32) % tb)
    blk_start = jnp.arange(n_blk, dtype=i32) * tb
    block_e = jnp.minimum(jnp.sum((pad_end[None, :] <= blk_start[:, None]).astype(i32), 1), n_exp - 1)
    n_used = pad_end[-1:] // tb + 1
    ys = _experts(xt, row_tok, row_dst, block_e, n_used, layer, w_gu, b_gu, w_down, b_down, tb, n_asg + tb)
    return _combine_ln(ys, gate_t.T, x, g, b, alpha=alpha, **split)


def _rope_tables(pos, rope):
    def cos_sin(d):
        inv = ROPE_THETA ** (-jnp.arange(0, d, 2, dtype=f32) / d)
        ang = pos.astype(f32)[:, None] * inv[None, :]
        return jnp.cos(ang), jnp.sin(ang)
    c, s = cos_sin(rope)
    mc = jnp.concatenate([c, c] * (LANES // rope), -1)
    ms = jnp.concatenate([-s, s] * (LANES // rope), -1)
    c, s = cos_sin(DIFF_ROT)
    n = pos.shape[0]
    dc = jnp.concatenate([c, c, jnp.ones((n, LANES - DIFF_ROT), f32)], -1)
    ds = jnp.concatenate([-s, s, jnp.zeros((n, LANES - DIFF_ROT), f32)], -1)
    return mc, ms, dc, ds


def kernel(x_prompt, x_sample, cache_mla_ckv, cache_mla_kpe, cache_diff_k, cache_diff_v, state_gla, meta_tokens, w_in_attn, mla_q_norm, mla_w_uq, mla_kv_norm, mla_w_uk, mla_w_uv, diff_lambda, diff_subln, w_out_attn, w_in_gla, gla_w_gate, gla_b_gate, gla_norm, w_out_gla, ln1_g, ln1_b, ln2_g, ln2_b, moe_w_router, moe_b_router, moe_w_gu, moe_b_gu, moe_w_down, moe_b_down):
    bp, seq, d = x_prompt.shape
    bs, ts, _ = x_sample.shape
    n_meta = meta_tokens.shape[0]
    past_len = cache_mla_ckv.shape[2]
    depth = ln1_g.shape[0]
    assert depth == 2 and ts == n_meta and seq % CHUNK == 0
    alpha = (2 * depth) ** 0.25

    n_heads = mla_w_uq.shape[2]
    q_lora = mla_w_uq.shape[1]
    kv_lora = mla_w_uk.shape[1]
    nope = mla_w_uk.shape[3]
    rope = mla_w_uq.shape[3] - nope
    vdim = mla_w_uv.shape[3]
    diff_heads = cache_diff_k.shape[3]
    diff_qk = cache_diff_k.shape[5]
    diff_cols = diff_heads * 2 * diff_qk
    assert nope == LANES and 2 * rope == LANES and diff_qk == LANES and n_heads % 2 == 0
    assert cache_diff_v.shape[4] == 2 * diff_qk and kv_lora % LANES == 0 and q_lora % LANES == 0

    n_f = bp * seq
    row_meta = n_f
    row_samp = n_f + bp * n_meta
    x = jnp.concatenate([x_prompt.reshape(n_f, d),
                         jnp.broadcast_to(meta_tokens.astype(f32)[None], (bp, n_meta, d)).reshape(bp * n_meta, d),
                         x_sample.reshape(bs * ts, d)], 0)
    pos = jnp.concatenate([jnp.tile(n_meta + jnp.arange(seq, dtype=i32), bp),
                           jnp.tile(jnp.arange(n_meta, dtype=i32), bp),
                           jnp.tile(past_len + jnp.arange(ts, dtype=i32), bs)])
    tq = _tile(seq, (512, 256, 128, 64))

    w_gu_b, w_down_b = moe_w_gu.astype(bf16), moe_w_down.astype(bf16)

    def moe(x, xt, layer, **split):
        return _moe_ln(x, xt, layer, moe_w_router[layer], moe_b_router[layer], w_gu_b, moe_b_gu, w_down_b,
                       moe_b_down, ln2_g[layer][None], ln2_b[layer][None], alpha=alpha, **split)

    w_in = w_in_attn[0]
    s1 = q_lora + kv_lora
    s2 = s1 + rope
    w_in_perm = jnp.concatenate([w_in[:, :s1], w_in[:, s2:], w_in[:, s1:s2],
                                 jnp.zeros((d, LANES - rope), f32)], 1).astype(bf16)
    y = _matmul(x, w_in_perm, f32, "attn_in_proj")
    wuq = mla_w_uq[0]
    wuq_perm = jnp.concatenate([wuq[:, :, :nope].reshape(q_lora, n_heads * nope),
                                wuq[:, :, nope:].reshape(q_lora, n_heads * rope)], 1).astype(bf16)
    wuk_t = jnp.transpose(mla_w_uk[0], (1, 2, 0)).astype(bf16)
    wuv = jnp.transpose(mla_w_uv[0], (1, 0, 2)).astype(bf16)
    tabs = _rope_tables(pos, rope)
    wuk_2 = jnp.transpose(mla_w_uk[0], (1, 0, 2)).astype(bf16)
    assert (bp * seq) % LANES == 0 and LANES % n_meta == 0
    qm, ckv, kpe, km, dqb, dk, dkb, dv, dvb, qt, kvt, dqt, dvt = _attn_prep(
        y, tabs, mla_q_norm[0][None], mla_kv_norm[0][None], wuq_perm, wuk_t, wuk_2,
        n_heads=n_heads, q_lora=q_lora, kv_lora=kv_lora, nope=nope, rope=rope, diff_cols=diff_cols,
        diff_heads=diff_heads)

    mla_scale = (nope + rope) ** -0.5
    diff_scale = diff_qk ** -0.5
    lambda_init = 0.8 - 0.6 * math.exp(-0.3 * 0)
    lam_vecs = diff_lambda[0].astype(f32)
    subln = diff_subln[0][None]
    common_d = dict(scale=diff_scale, lambda_init=lambda_init)
    o_mla = jnp.concatenate([
        _mla_flash(qt, km, kvt, wuv, n_batch=bp, seq=seq, n_meta=n_meta, scale=mla_scale, tq=tq),
        _mla_small(qm, km, wuv, None, n_seq=bp, tq=n_meta, row0=row_meta, scale=mla_scale, kv_lora=kv_lora),
        _mla_small(qm, km, wuv, (cache_mla_ckv[0], cache_mla_kpe[0]), n_seq=bs, tq=ts, row0=row_samp,
                   scale=mla_scale, kv_lora=kv_lora)], 0)
    past_d = (cache_diff_k[0].reshape(bs, past_len, diff_cols), cache_diff_v[0].reshape(bs, past_len, diff_cols))
    o_dif = jnp.concatenate([
        _diff_flash(dqt, dkb, dvt, lam_vecs, subln, n_batch=bp, seq=seq, n_meta=n_meta, tq=tq, **common_d),
        _diff_small(dqb, dkb, dvb, lam_vecs, subln, None, n_seq=bp, tq=n_meta, row0=row_meta, **common_d),
        _diff_small(dqb, dkb, dvb, lam_vecs, subln, past_d, n_seq=bs, tq=ts, row0=row_samp, **common_d)], 0)
    w_out = w_out_attn[0].astype(bf16)
    mla_out = n_heads * vdim
    x, xt = _proj_ln([o_mla, o_dif], [w_out[:mla_out], w_out[mla_out:]], x, ln1_g[0][None], ln1_b[0][None],
                     alpha=alpha)
    x = moe(x, xt, 0)

    gla_heads = state_gla.shape[2]
    gla_dk, gla_dv = state_gla.shape[3], state_gla.shape[4]
    qk_cols, v_cols = gla_heads * gla_dk, gla_heads * gla_dv
    rank = gla_w_gate.shape[1]
    w_gla = jnp.concatenate([w_in_gla[0], jnp.zeros((d, LANES - rank), f32)], 1).astype(bf16)
    yg = _matmul(x, w_gla, f32, "gla_in_proj")
    w_gate = jnp.concatenate([gla_w_gate[0], jnp.zeros((LANES - rank, qk_cols), f32)], 0)
    b_gate = gla_b_gate[0][None]
    g_norm = gla_norm[0][None]
    gla_kw = dict(n_heads=gla_heads, qk_cols=qk_cols, v_cols=v_cols)
    s_init = jnp.concatenate([jnp.zeros((bp,) + state_gla.shape[2:], f32), state_gla[0].astype(f32)], 0)
    o_short, s_short = _gla(yg, w_gate, b_gate, g_norm, s_init, n_seq=bp + bs, n_chunks=1, chunk=n_meta,
                            row0=row_meta, **gla_kw)
    o_frames, s_frames = _gla(yg, w_gate, b_gate, g_norm, s_short[:bp], n_seq=bp, n_chunks=seq // GLA_CHUNK,
                              chunk=GLA_CHUNK, row0=0, **gla_kw)
    o_gla = jnp.concatenate([o_frames, o_short], 0)
    x, xt = _proj_ln([o_gla], [w_out_gla[0].astype(bf16)], x, ln1_g[1][None], ln1_b[1][None], alpha=alpha)
    y_frames, y_samp = moe(x, xt, 1, head_rows=n_f, tail_row0=row_samp)

    def rows_p(a, tail):
        c = a.shape[1]
        return jnp.concatenate([a[row_meta:row_samp].reshape(bp, n_meta, c), a[:n_f].reshape(bp, seq, c)],
                               1).reshape((1, bp, n_meta + seq) + tail)

    def rows_s(a, tail):
        return a[row_samp:].reshape((1, bs, ts) + tail)

    kpe = kpe[:, :rope]
    t_dk, t_dv = (diff_heads, 2, diff_qk), (diff_heads, 2 * diff_qk)
    return (y_frames.reshape(bp, seq, d), y_samp.reshape(bs, ts, d),
            rows_p(ckv, (kv_lora,)), rows_p(kpe, (rope,)), rows_p(dk, t_dk), rows_p(dv, t_dv), s_frames[None],
            rows_s(ckv, (kv_lora,)), rows_s(kpe, (rope,)), rows_s(dk, t_dk), rows_s(dv, t_dv), s_short[bp:][None])
```
